```python
import math
import jax
import jax.numpy as jnp
from jax import lax
import numpy as np

D_MODEL = 1024
BATCH = 32
SEQ = 2048
DEPTH = 2
DEC_BATCH = 2
DEC_SEQ = 16384
PAST_LEN = 128

GRID_W = 64
Q_BLOCK = 128
NORM_EPS = 1e-6
ROPE_THETA = 10000.0

SSM_WIDTH = 256
SSM_GROUP = 16
SSM_GROUPS = SSM_WIDTH // SSM_GROUP
SSM_STATE = 64
SSM_DT_MIN = 1e-3
SSM_DT_MAX = 1e-1

MLA_HEADS = 6
MLA_NOPE = 64
MLA_ROPE = 32
MLA_V = 64
MLA_Q_LORA = 256
MLA_KV_LORA = 128
MLA_WIDTH = MLA_HEADS * MLA_V

GQA_HEADS = 6
GQA_KV_HEADS = 2
GQA_HEAD_DIM = 64
GQA_WIDTH = GQA_HEADS * GQA_HEAD_DIM

MIX_WIDTH = SSM_WIDTH + MLA_WIDTH + GQA_WIDTH
IN_SPLITS = (SSM_WIDTH, MLA_Q_LORA, MLA_KV_LORA, MLA_ROPE,
             GQA_WIDTH, GQA_KV_HEADS * GQA_HEAD_DIM, GQA_KV_HEADS * GQA_HEAD_DIM)
IN_WIDTH = 1312

N_EXPERT_GROUPS = 4
EXPERTS_PER_GROUP = 8
N_EXPERTS = N_EXPERT_GROUPS * EXPERTS_PER_GROUP
TOP_K = 2
EXPERT_FF = 512
MOE_BLOCK = 128

kernel_name = "hymba_s5_mla_gqa_hmoe_encoder"


def rms_norm(x, gain):
    x32 = x.astype(jnp.float32)
    y = x32 * lax.rsqrt(jnp.mean(jnp.square(x32), axis=-1, keepdims=True) + NORM_EPS)
    return (y * gain.astype(jnp.float32)).astype(x.dtype)


def axial_rope_tables(seq_len, rot_dim):
    rows = seq_len // GRID_W
    row_idx, col_idx = jnp.meshgrid(jnp.arange(rows), jnp.arange(GRID_W), indexing='ij')
    row_idx = row_idx.reshape(-1).astype(jnp.float32)
    col_idx = col_idx.reshape(-1).astype(jnp.float32)
    n_freq = rot_dim // 4
    inv_freq = ROPE_THETA ** (-jnp.arange(n_freq, dtype=jnp.float32) / n_freq)
    ang = jnp.concatenate([row_idx[:, None] * inv_freq, col_idx[:, None] * inv_freq], axis=-1)
    return jnp.cos(ang), jnp.sin(ang)


def apply_rope(x, rope):
    cos, sin = rope
    x32 = x.astype(jnp.float32)
    x1, x2 = jnp.split(x32, 2, axis=-1)
    c = cos[None, :, None, :]
    s = sin[None, :, None, :]
    return jnp.concatenate([x1 * c - x2 * s, x2 * c + x1 * s], axis=-1).astype(x.dtype)


def block_attention(q, k, v, scale):
    bsz, seq_len, n_kv, rep, dk = q.shape
    n_blocks = seq_len // Q_BLOCK
    q_blocks = jnp.moveaxis(q.reshape(bsz, n_blocks, Q_BLOCK, n_kv, rep, dk), 1, 0)

    def one_block(q_blk):
        s = jnp.einsum('bqhrd,bkhd->bhrqk', q_blk, k, preferred_element_type=jnp.float32) * scale
        p = jax.nn.softmax(s, axis=-1)
        return jnp.einsum('bhrqk,bkhd->bqhrd', p.astype(v.dtype), v)

    out = lax.map(one_block, q_blocks)
    return jnp.moveaxis(out, 0, 1).reshape(bsz, seq_len, -1)


def _linear_recurrence(left, right):
    a_l, b_l = left
    a_r, b_r = right
    return a_l * a_r, a_r * b_l + b_r


def s5_bidirectional(u, lam_re, lam_im, log_dt, b_re, b_im, c_re, c_im, d_skip, w_glu, b_glu):
    bsz, seq_len, _ = u.shape
    u32 = u.astype(jnp.float32).reshape(bsz, seq_len, SSM_GROUPS, SSM_GROUP)
    y = d_skip.astype(jnp.float32).reshape(SSM_GROUPS, SSM_GROUP) * u32
    uc = u32.astype(jnp.complex64)
    for direction in range(2):
        lam = lax.complex(lam_re[direction].astype(jnp.float32), lam_im[direction].astype(jnp.float32))
        dt = jnp.exp(log_dt[direction].astype(jnp.float32))[:, None]
        lam_bar = jnp.exp(lam * dt)
        b_mat = lax.complex(b_re[direction].astype(jnp.float32), b_im[direction].astype(jnp.float32))
        b_bar = ((lam_bar - 1.0) / lam)[..., None] * b_mat
        bu = jnp.einsum('blgc,gpc->blgp', uc, b_bar)
        a = jnp.broadcast_to(lam_bar, bu.shape)
        _, h = lax.associative_scan(_linear_recurrence, (a, bu), reverse=(direction == 1), axis=1)
        c_mat = lax.complex(c_re[direction].astype(jnp.float32), c_im[direction].astype(jnp.float32))
        y = y + jnp.einsum('blgp,gcp->blgc', h, c_mat).real
    y = jax.nn.gelu(y.reshape(bsz, seq_len, SSM_WIDTH))
    gate = jax.nn.sigmoid(y @ w_glu.astype(jnp.float32) + b_glu.astype(jnp.float32))
    return (y * gate).astype(u.dtype)


def parallel_mixer(xn, rope_mla, rope_gqa, w_in,
                   s5_lam_re, s5_lam_im, s5_log_dt, s5_b_re, s5_b_im, s5_c_re, s5_c_im,
                   s5_d, s5_w_glu, s5_b_glu,
                   mla_q_norm, mla_w_uq, mla_kv_norm, mla_w_ukv,
                   gqa_q_norm, gqa_k_norm,
                   out_norm_ssm, out_norm_mla, out_norm_gqa, w_out):
    bsz, seq_len, _ = xn.shape
    proj = xn @ w_in
    offsets = np.cumsum(IN_SPLITS)[:-1].tolist()
    u, c_q, c_kv, k_pe, q_g, k_g, v_g = jnp.split(proj, offsets, axis=-1)

    o_ssm = s5_bidirectional(u, s5_lam_re, s5_lam_im, s5_log_dt, s5_b_re, s5_b_im,
                             s5_c_re, s5_c_im, s5_d, s5_w_glu, s5_b_glu)

    q = (rms_norm(c_q, mla_q_norm) @ mla_w_uq).reshape(bsz, seq_len, MLA_HEADS, MLA_NOPE + MLA_ROPE)
    q_nope, q_pe = jnp.split(q, [MLA_NOPE], axis=-1)
    q_pe = apply_rope(q_pe, rope_mla)
    kv = (rms_norm(c_kv, mla_kv_norm) @ mla_w_ukv).reshape(bsz, seq_len, MLA_HEADS, MLA_NOPE + MLA_V)
    k_nope, v_mla = jnp.split(kv, [MLA_NOPE], axis=-1)
    k_pe = apply_rope(k_pe[:, :, None, :], rope_mla)
    q_mla = jnp.concatenate([q_nope, q_pe], axis=-1)[:, :, :, None, :]
    k_mla = jnp.concatenate(
        [k_nope, jnp.broadcast_to(k_pe, (bsz, seq_len, MLA_HEADS, MLA_ROPE))], axis=-1)
    o_mla = block_attention(q_mla, k_mla, v_mla, (MLA_NOPE + MLA_ROPE) ** -0.5)

    q_c = rms_norm(q_g.reshape(bsz, seq_len, GQA_HEADS, GQA_HEAD_DIM), gqa_q_norm)
    k_c = rms_norm(k_g.reshape(bsz, seq_len, GQA_KV_HEADS, GQA_HEAD_DIM), gqa_k_norm)
    q_c = apply_rope(q_c, rope_gqa).reshape(
        bsz, seq_len, GQA_KV_HEADS, GQA_HEADS // GQA_KV_HEADS, GQA_HEAD_DIM)
    k_c = apply_rope(k_c, rope_gqa)
    v_c = v_g.reshape(bsz, seq_len, GQA_KV_HEADS, GQA_HEAD_DIM)
    o_gqa = block_attention(q_c, k_c, v_c, GQA_HEAD_DIM ** -0.5)

    mixed = jnp.concatenate([rms_norm(o_ssm, out_norm_ssm),
                             rms_norm(o_mla, out_norm_mla),
                             rms_norm(o_gqa, out_norm_gqa)], axis=-1)
    return mixed @ w_out


def routed_experts(xf, expert_idx, weights, w_gate, w_up, w_down):
    n_tok, d = xf.shape
    n_assign = n_tok * TOP_K
    flat_e = expert_idx.reshape(-1)
    order = jnp.argsort(flat_e)
    sorted_e = flat_e[order]
    tok = order // TOP_K
    counts = jnp.bincount(flat_e, length=N_EXPERTS)
    padded = (counts + MOE_BLOCK - 1) // MOE_BLOCK * MOE_BLOCK
    pad_end = jnp.cumsum(padded)
    pad_start = pad_end - padded
    start = jnp.cumsum(counts) - counts
    dest = pad_start[sorted_e] + jnp.arange(n_assign) - start[sorted_e]
    n_blocks = -(-n_assign // MOE_BLOCK) + N_EXPERTS
    buf = jnp.zeros((n_blocks * MOE_BLOCK, d), xf.dtype).at[dest].set(xf[tok])
    block_expert = jnp.minimum(
        jnp.searchsorted(pad_end, jnp.arange(n_blocks) * MOE_BLOCK, side='right'), N_EXPERTS - 1)

    def expert_block(args):
        xb, e = args
        h = jax.nn.silu(xb @ w_gate[e]) * (xb @ w_up[e])
        return h @ w_down[e]

    out = lax.map(expert_block, (buf.reshape(n_blocks, MOE_BLOCK, d), block_expert))
    out = out.reshape(-1, d)[dest] * weights.reshape(-1)[order][:, None].astype(xf.dtype)
    return jnp.zeros_like(xf).at[tok].add(out)


def hierarchical_moe(xn, rg_w, rg_b, re_w, re_b, w_gate, w_up, w_down):
    bsz, seq_len, d = xn.shape
    xf = xn.reshape(-1, d)
    n_tok = xf.shape[0]
    g_prob = jax.nn.softmax((xf @ rg_w).astype(jnp.float32) + rg_b.astype(jnp.float32), axis=-1)
    g_w, g_idx = lax.top_k(g_prob, 1)
    e_logits = ((xf @ re_w).astype(jnp.float32) + re_b.astype(jnp.float32)).reshape(
        n_tok, N_EXPERT_GROUPS, EXPERTS_PER_GROUP)
    e_logits = e_logits[jnp.arange(n_tok), g_idx[:, 0]]
    e_prob = jax.nn.softmax(e_logits, axis=-1)
    top_p, top_local = lax.top_k(e_prob, TOP_K)
    weights = g_w * top_p / jnp.sum(top_p, axis=-1, keepdims=True)
    expert_idx = g_idx * EXPERTS_PER_GROUP + top_local
    y = routed_experts(xf, expert_idx, weights, w_gate, w_up, w_down)
    return y.reshape(bsz, seq_len, d)


def setup_inputs(seed: int = 0):
    key = jax.random.key(seed)
    ks = jax.random.split(key, 40)
    counter = iter(range(40))
    f32 = jnp.float32

    def nrm(shape, scale):
        return jax.random.normal(ks[next(counter)], shape, f32) * scale

    def gain(shape):
        return 1.0 + nrm(shape, 0.05)

    x_prompt = nrm((BATCH, SEQ, D_MODEL), 1.0)
    x_sample = nrm((DEC_BATCH, DEC_SEQ, D_MODEL), 1.0)
    attn_norm = gain((DEPTH, D_MODEL))
    w_in = nrm((DEPTH, D_MODEL, IN_WIDTH), D_MODEL ** -0.5)
    lam_shape = (DEPTH, 2, SSM_GROUPS, SSM_STATE)
    s5_lam_re = -0.5 * jnp.exp(nrm(lam_shape, 0.05))
    s5_lam_im = math.pi * jnp.arange(SSM_STATE, dtype=f32) + nrm(lam_shape, 0.05)
    s5_log_dt = jax.random.uniform(ks[next(counter)], (DEPTH, 2, SSM_GROUPS), dtype=f32,
                                   minval=math.log(SSM_DT_MIN), maxval=math.log(SSM_DT_MAX))
    b_scale = (2.0 * SSM_GROUP) ** -0.5
    s5_b_re = nrm((DEPTH, 2, SSM_GROUPS, SSM_STATE, SSM_GROUP), b_scale)
    s5_b_im = nrm((DEPTH, 2, SSM_GROUPS, SSM_STATE, SSM_GROUP), b_scale)
    c_scale = 0.5 ** 0.5
    s5_c_re = nrm((DEPTH, 2, SSM_GROUPS, SSM_GROUP, SSM_STATE), c_scale)
    s5_c_im = nrm((DEPTH, 2, SSM_GROUPS, SSM_GROUP, SSM_STATE), c_scale)
    s5_d = nrm((DEPTH, SSM_WIDTH), 1.0)
    s5_w_glu = nrm((DEPTH, SSM_WIDTH, SSM_WIDTH), SSM_WIDTH ** -0.5)
    s5_b_glu = nrm((DEPTH, SSM_WIDTH), 0.01)
    mla_q_norm = gain((DEPTH, MLA_Q_LORA))
    mla_w_uq = nrm((DEPTH, MLA_Q_LORA, MLA_HEADS * (MLA_NOPE + MLA_ROPE)), MLA_Q_LORA ** -0.5)
    mla_kv_norm = gain((DEPTH, MLA_KV_LORA))
    mla_w_ukv = nrm((DEPTH, MLA_KV_LORA, MLA_HEADS * (MLA_NOPE + MLA_V)), MLA_KV_LORA ** -0.5)
    gqa_q_norm = gain((DEPTH, GQA_HEAD_DIM))
    gqa_k_norm = gain((DEPTH, GQA_HEAD_DIM))
    out_norm_ssm = gain((DEPTH, SSM_WIDTH))
    out_norm_mla = gain((DEPTH, MLA_WIDTH))
    out_norm_gqa = gain((DEPTH, GQA_WIDTH))
    w_out = nrm((DEPTH, MIX_WIDTH, D_MODEL), MIX_WIDTH ** -0.5)
    ffn_norm = gain((DEPTH, D_MODEL))
    router_group_w = nrm((DEPTH, D_MODEL, N_EXPERT_GROUPS), D_MODEL ** -0.5)
    router_group_b = nrm((DEPTH, N_EXPERT_GROUPS), 0.01)
    router_expert_w = nrm((DEPTH, D_MODEL, N_EXPERTS), D_MODEL ** -0.5)
    router_expert_b = nrm((DEPTH, N_EXPERTS), 0.01)
    expert_w_gate = nrm((DEPTH, N_EXPERTS, D_MODEL, EXPERT_FF), D_MODEL ** -0.5)
    expert_w_up = nrm((DEPTH, N_EXPERTS, D_MODEL, EXPERT_FF), D_MODEL ** -0.5)
    expert_w_down = nrm((DEPTH, N_EXPERTS, EXPERT_FF, D_MODEL), EXPERT_FF ** -0.5)
    final_norm = gain((D_MODEL,))
    return dict(x_prompt=x_prompt, x_sample=x_sample, attn_norm=attn_norm, w_in=w_in,
                s5_lam_re=s5_lam_re, s5_lam_im=s5_lam_im, s5_log_dt=s5_log_dt,
                s5_b_re=s5_b_re, s5_b_im=s5_b_im, s5_c_re=s5_c_re, s5_c_im=s5_c_im,
                s5_d=s5_d, s5_w_glu=s5_w_glu, s5_b_glu=s5_b_glu,
                mla_q_norm=mla_q_norm, mla_w_uq=mla_w_uq, mla_kv_norm=mla_kv_norm, mla_w_ukv=mla_w_ukv,
                gqa_q_norm=gqa_q_norm, gqa_k_norm=gqa_k_norm,
                out_norm_ssm=out_norm_ssm, out_norm_mla=out_norm_mla, out_norm_gqa=out_norm_gqa,
                w_out=w_out, ffn_norm=ffn_norm,
                router_group_w=router_group_w, router_group_b=router_group_b,
                router_expert_w=router_expert_w, router_expert_b=router_expert_b,
                expert_w_gate=expert_w_gate, expert_w_up=expert_w_up, expert_w_down=expert_w_down,
                final_norm=final_norm)


def reference(x_prompt, x_sample, attn_norm, w_in,
              s5_lam_re, s5_lam_im, s5_log_dt, s5_b_re, s5_b_im, s5_c_re, s5_c_im,
              s5_d, s5_w_glu, s5_b_glu,
              mla_q_norm, mla_w_uq, mla_kv_norm, mla_w_ukv,
              gqa_q_norm, gqa_k_norm,
              out_norm_ssm, out_norm_mla, out_norm_gqa, w_out, ffn_norm,
              router_group_w, router_group_b, router_expert_w, router_expert_b,
              expert_w_gate, expert_w_up, expert_w_down, final_norm):
    def trunk(x):
        seq_len = x.shape[1]
        rope_mla = axial_rope_tables(seq_len, MLA_ROPE)
        rope_gqa = axial_rope_tables(seq_len, GQA_HEAD_DIM)
        for l in range(DEPTH):
            h = rms_norm(x, attn_norm[l])
            x = x + parallel_mixer(
                h, rope_mla, rope_gqa, w_in[l],
                s5_lam_re[l], s5_lam_im[l], s5_log_dt[l], s5_b_re[l], s5_b_im[l],
                s5_c_re[l], s5_c_im[l], s5_d[l], s5_w_glu[l], s5_b_glu[l],
                mla_q_norm[l], mla_w_uq[l], mla_kv_norm[l], mla_w_ukv[l],
                gqa_q_norm[l], gqa_k_norm[l],
                out_norm_ssm[l], out_norm_mla[l], out_norm_gqa[l], w_out[l])
            h = rms_norm(x, ffn_norm[l])
            x = x + hierarchical_moe(
                h, router_group_w[l], router_group_b[l], router_expert_w[l], router_expert_b[l],
                expert_w_gate[l], expert_w_up[l], expert_w_down[l])
        return rms_norm(x, final_norm)

    y_prompt = trunk(x_prompt)
    y_sample = trunk(x_sample)
    return (y_prompt, y_sample)
```

```python
import functools
import math

import jax
import jax.numpy as jnp
import numpy as np
from jax import lax
from jax.experimental import pallas as pl
from jax.experimental.pallas import tpu as pltpu

F32 = jnp.float32
BF16 = jnp.bfloat16

D_MODEL = 1024
GRID_W = 64
NORM_EPS = 1e-6
ROPE_THETA = 10000.0

SSM_WIDTH = 256
SSM_GROUP = 16
SSM_GROUPS = 16
SSM_STATE = 64

MLA_HEADS = 6
MLA_NOPE = 64
MLA_ROPE = 32
MLA_V = 64
MLA_Q_LORA = 256
MLA_KV_LORA = 128
MLA_WIDTH = MLA_HEADS * MLA_V

GQA_HEADS = 6
GQA_KV_HEADS = 2
GQA_HEAD_DIM = 64
GQA_WIDTH = GQA_HEADS * GQA_HEAD_DIM

N_EXPERT_GROUPS = 4
EXPERTS_PER_GROUP = 8
N_EXPERTS = 32
EXPERT_FF = 512

LANE = 128
SLOT = 128
S5_CHUNK = 64
S5_K = S5_CHUNK * SSM_GROUP
S5_SW = 512
S5_HW = 256
WIN_PAD = 1536
MOE_TILE = 256
VMEM_LIMIT = 48 * 1024 * 1024

_C_U, _C_CQ, _C_CKV, _C_KPE, _C_KPER, _C_QG, _C_KG, _C_VG = 0, 256, 512, 640, 768, 896, 1280, 1408


def _cparams(*sem):
    return pltpu.CompilerParams(dimension_semantics=sem, vmem_limit_bytes=VMEM_LIMIT)


def _split_bf16(x):
    hi = x.astype(BF16)
    lo = (x - hi.astype(F32)).astype(BF16)
    return hi, lo


def _dot(a, b):
    return jnp.dot(a, b, preferred_element_type=F32)


def _dot2(x, w):
    hi, lo = _split_bf16(x)
    return _dot(hi, w) + _dot(lo, w)


def _rms(x, gain):
    return x * lax.rsqrt(jnp.mean(x * x, axis=-1, keepdims=True) + NORM_EPS) * gain


def _inproj_kernel(x_ref, g_ref, win_ref, qng_ref, wuq_ref, wuqr_ref, kvng_ref, wukk_ref, wukv_ref,
                   gqn_ref, gkn_ref, bd_ref, perm_ref, place_ref,
                   mc_ref, ms_ref, gc_ref, gs_ref,
                   u_ref, qm_ref, km_ref, vm_ref, qg_ref, kg_ref, vg_ref):
    h = _rms(x_ref[...], g_ref[...]).astype(BF16)
    proj = _dot(h, win_ref[...])
    u_ref[...] = proj[:, _C_U:_C_U + SSM_WIDTH].astype(BF16)
    vg_ref[...] = proj[:, _C_VG:_C_VG + 128].astype(BF16)

    mc = mc_ref[...]
    ms = ms_ref[...]
    cq = _rms(proj[:, _C_CQ:_C_CQ + MLA_Q_LORA], qng_ref[...]).astype(BF16)
    qa = _dot(cq, wuq_ref[...])
    qb = _dot(cq, wuqr_ref[...])
    scale = (MLA_NOPE + MLA_ROPE) ** -0.5
    for hd in range(MLA_HEADS):
        sl = slice(hd * SLOT, (hd + 1) * SLOT)
        qm_ref[:, sl] = ((qa[:, sl] * mc + qb[:, sl] * ms) * scale).astype(BF16)
    ckv = _rms(proj[:, _C_CKV:_C_CKV + MLA_KV_LORA], kvng_ref[...]).astype(BF16)
    kk = _dot(ckv, wukk_ref[...])
    kpe = proj[:, _C_KPE:_C_KPE + SLOT] * mc + proj[:, _C_KPER:_C_KPER + SLOT] * ms
    for hd in range(MLA_HEADS):
        sl = slice(hd * SLOT, (hd + 1) * SLOT)
        km_ref[:, sl] = (kk[:, sl] + kpe).astype(BF16)
    vm_ref[...] = _dot(ckv, wukv_ref[...]).astype(BF16)

    gc = gc_ref[...]
    gs = gs_ref[...]
    bd = bd_ref[...]
    perm = perm_ref[...]
    qg = proj[:, _C_QG:_C_QG + GQA_WIDTH]
    qn = qg * lax.rsqrt(_dot2(qg * qg, bd) + NORM_EPS) * gqn_ref[...]
    gc3 = jnp.concatenate([gc, gc, gc], axis=-1)
    gs3 = jnp.concatenate([gs, gs, gs], axis=-1)
    qr = (qn * gc3 + _dot2(qn, perm) * gs3) * (GQA_HEAD_DIM ** -0.5)
    qg_ref[...] = _dot(qr.astype(BF16), place_ref[...]).astype(BF16)
    kg = proj[:, _C_KG:_C_KG + 128]
    kn = kg * lax.rsqrt(_dot2(kg * kg, bd[:128, :128]) + NORM_EPS) * gkn_ref[...]
    kg_ref[...] = (kn * gc + _dot2(kn, perm[:128, :128]) * gs).astype(BF16)


def _inproj(x, lw, tabs, seq_len):
    n_tok = x.shape[0]
    tm = min(512, seq_len)
    nt = n_tok // tm
    per_seq = seq_len // tm

    def row(i):
        return (i, 0)

    def full(i):
        return (0, 0)

    def tab(i):
        return (i % per_seq, 0)

    consts = [lw['attn_norm'], lw['w_in'], lw['mla_q_norm'], lw['w_uq'], lw['w_uq_rot'],
              lw['mla_kv_norm'], lw['w_ukv_k'], lw['w_ukv_v'], lw['gqa_q_norm'], lw['gqa_k_norm'],
              lw['bd'], lw['perm'], lw['place']]
    in_specs = [pl.BlockSpec((tm, D_MODEL), row)]
    in_specs += [pl.BlockSpec(c.shape, full) for c in consts]
    in_specs += [pl.BlockSpec((tm, LANE), tab)] * 4
    widths = [SSM_WIDTH, MLA_HEADS * SLOT, MLA_HEADS * SLOT, MLA_WIDTH, GQA_HEADS * SLOT, 128, 128]
    return pl.pallas_call(
        _inproj_kernel,
        grid=(nt,),
        in_specs=in_specs,
        out_specs=[pl.BlockSpec((tm, w), row) for w in widths],
        out_shape=[jax.ShapeDtypeStruct((n_tok, w), BF16) for w in widths],
        compiler_params=_cparams("parallel"),
        name="inproj",
    )(x, *consts, *tabs)


def _s5_state_kernel(u_ref, w_ref, s_ref):
    s_ref[...] = _dot(u_ref[0], w_ref[0])


def _s5_state(u_t, w_s):
    n_grp, rows, _ = u_t.shape
    tr = min(512, rows)
    return pl.pallas_call(
        _s5_state_kernel,
        grid=(n_grp, rows // tr),
        in_specs=[pl.BlockSpec((1, tr, S5_K), lambda g, r: (g, r, 0)),
                  pl.BlockSpec((1, S5_K, S5_SW), lambda g, r: (g, 0, 0))],
        out_specs=pl.BlockSpec((tr, S5_SW), lambda g, r: (r, g)),
        out_shape=jax.ShapeDtypeStruct((rows, n_grp * S5_SW), F32),
        compiler_params=_cparams("parallel", "parallel"),
        name="s5_state",
    )(u_t, w_s)


def _s5_carry_kernel(s_ref, c_ref, h_ref, *, n_chunks):
    a1f, a2f, a3f = c_ref[0, 0:1, 0:LANE], c_ref[0, 1:2, 0:LANE], c_ref[0, 2:3, 0:LANE]
    a1b, a2b, a3b = c_ref[0, 0:1, LANE:], c_ref[0, 1:2, LANE:], c_ref[0, 2:3, LANE:]
    bsz = s_ref.shape[1]
    zero = jnp.zeros((bsz, LANE), F32)

    def body(i, carry):
        hf, hfs, hb, hbs = carry
        jb = n_chunks - 1 - i
        h_ref[i, :, 0:LANE] = hf
        h_ref[jb, :, LANE:2 * LANE] = hb
        sf = s_ref[i, :, 0:LANE]
        sfs = s_ref[i, :, LANE:2 * LANE]
        sb = s_ref[jb, :, 2 * LANE:3 * LANE]
        sbs = s_ref[jb, :, 3 * LANE:4 * LANE]
        return (a1f * hf + a2f * hfs + sf, a1f * hfs + a3f * hf + sfs,
                a1b * hb + a2b * hbs + sb, a1b * hbs + a3b * hb + sbs)

    lax.fori_loop(0, n_chunks, body, (zero, zero, zero, zero))


def _s5_carry(s, coef, n_chunks, bsz):
    n_grp = coef.shape[0]
    s3 = s.reshape(n_chunks, bsz, n_grp * S5_SW)
    h = pl.pallas_call(
        functools.partial(_s5_carry_kernel, n_chunks=n_chunks),
        grid=(n_grp,),
        in_specs=[pl.BlockSpec((n_chunks, bsz, S5_SW), lambda g: (0, 0, g)),
                  pl.BlockSpec((1, 3, S5_HW), lambda g: (g, 0, 0))],
        out_specs=pl.BlockSpec((n_chunks, bsz, S5_HW), lambda g: (0, 0, g)),
        out_shape=jax.ShapeDtypeStruct((n_chunks, bsz, n_grp * S5_HW), F32),
        compiler_params=_cparams("parallel"),
        name="s5_carry",
    )(s3, coef)
    return h.reshape(n_chunks * bsz, n_grp * S5_HW)


def _gelu_tanh(y):
    return 0.5 * y * (1.0 + jnp.tanh(math.sqrt(2.0 / math.pi) * (y + 0.044715 * (y * y * y))))


def _s5_out_kernel(u_ref, h_ref, toep_ref, wy_ref, y_ref):
    y = _dot(u_ref[0], toep_ref[0]) + _dot(h_ref[...].astype(BF16), wy_ref[0])
    y_ref[0] = _gelu_tanh(y)


def _s5_out(u_t, h_in, toep, w_y):
    n_grp, rows, _ = u_t.shape
    tr = min(512, rows)
    return pl.pallas_call(
        _s5_out_kernel,
        grid=(n_grp, rows // tr),
        in_specs=[pl.BlockSpec((1, tr, S5_K), lambda g, r: (g, r, 0)),
                  pl.BlockSpec((tr, S5_HW), lambda g, r: (r, g)),
                  pl.BlockSpec((1, S5_K, S5_K), lambda g, r: (g, 0, 0)),
                  pl.BlockSpec((1, S5_HW, S5_K), lambda g, r: (g, 0, 0))],
        out_specs=pl.BlockSpec((1, tr, S5_K), lambda g, r: (g, r, 0)),
        out_shape=jax.ShapeDtypeStruct((n_grp, rows, S5_K), F32),
        compiler_params=_cparams("parallel", "parallel"),
        name="s5_out",
    )(u_t, h_in, toep, w_y)


def _s5(u, lw, bsz, seq_len):
    n_chunks = seq_len // S5_CHUNK
    u_t = u.reshape(bsz, n_chunks, S5_CHUNK, SSM_GROUPS, SSM_GROUP)
    u_t = u_t.transpose(3, 1, 0, 2, 4).reshape(SSM_GROUPS, n_chunks * bsz, S5_K)
    s = _s5_state(u_t, lw['s5_w_s'])
    h_in = _s5_carry(s, lw['s5_coef'], n_chunks, bsz)
    y_t = _s5_out(u_t, h_in, lw['s5_toep'], lw['s5_w_y'])
    y = y_t.reshape(SSM_GROUPS, n_chunks, bsz, S5_CHUNK, SSM_GROUP).transpose(2, 1, 3, 0, 4)
    return y.reshape(bsz * seq_len, SSM_WIDTH)


def _attn_kernel(q_ref, k_ref, v_ref, o_ref, *, heads, kslot, voff, tk, n_kv):
    tq = q_ref.shape[0]
    outs = []
    for j in range(heads):
        q = q_ref[:, j * SLOT:(j + 1) * SLOT]
        ks0 = kslot[j] * SLOT

        def body(i, carry, q=q, ks0=ks0):
            m, l, acc = carry
            start = pl.multiple_of(i * tk, tk)
            kb = k_ref[pl.ds(start, tk), ks0:ks0 + SLOT]
            vb = v_ref[pl.ds(start, tk), :]
            s = lax.dot_general(q, kb, (((1,), (1,)), ((), ())), preferred_element_type=F32)
            m_new = jnp.maximum(m, jnp.max(s, axis=-1, keepdims=True))
            alpha = jnp.exp(m - m_new)
            p = jnp.exp(s - m_new)
            l = alpha * l + jnp.sum(p, axis=-1, keepdims=True)
            acc = alpha * acc + _dot(p.astype(BF16), vb)
            return m_new, l, acc

        m0 = jnp.full((tq, 1), -jnp.inf, F32)
        l0 = jnp.zeros((tq, 1), F32)
        a0 = jnp.zeros((tq, SLOT), F32)
        _, l, acc = lax.fori_loop(0, n_kv, body, (m0, l0, a0))
        outs.append((acc / l)[:, voff[j]:voff[j] + 64])
    o_ref[...] = jnp.concatenate(outs, axis=-1)


def _attention(q, k, v, bsz, seq_len, *, groups, heads, kslots, kslot, voff):
    tq = min(256, seq_len)
    tk = min(512, seq_len)
    nq = seq_len // tq
    q3 = q.reshape(bsz, seq_len, -1)
    k3 = k.reshape(bsz, seq_len, -1)
    v3 = v.reshape(bsz, seq_len, -1)
    ow = heads * 64
    out = pl.pallas_call(
        functools.partial(_attn_kernel, heads=heads, kslot=kslot, voff=voff, tk=tk, n_kv=seq_len // tk),
        grid=(bsz, groups, nq),
        in_specs=[pl.BlockSpec((None, tq, heads * SLOT), lambda b, g, i: (b, i, g)),
                  pl.BlockSpec((None, seq_len, kslots * SLOT), lambda b, g, i: (b, 0, g)),
                  pl.BlockSpec((None, seq_len, SLOT), lambda b, g, i: (b, 0, g))],
        out_specs=pl.BlockSpec((None, tq, ow), lambda b, g, i: (b, i, g)),
        out_shape=jax.ShapeDtypeStruct((bsz, seq_len, groups * ow), F32),
        compiler_params=_cparams("parallel", "parallel", "arbitrary"),
        name="attention",
    )(q3, k3, v3)
    return out.reshape(bsz * seq_len, groups * ow)


def _outproj_kernel(y_ref, om_ref, og_ref, x_ref, wglu_ref, bglu_ref, ns_ref, nm_ref, ng_ref,
                    wos_ref, wom_ref, wog_ref, fn_ref, wrh_ref, wrl_ref, rb_ref,
                    xo_ref, h_ref, idx_ref, wt_ref):
    y = y_ref[...]
    gate = jax.nn.sigmoid(_dot(y.astype(BF16), wglu_ref[...]) + bglu_ref[...])
    o_s = _rms(y * gate, ns_ref[...]).astype(BF16)
    o_m = _rms(om_ref[...], nm_ref[...]).astype(BF16)
    o_g = _rms(og_ref[...], ng_ref[...]).astype(BF16)
    x_new = x_ref[...] + _dot(o_s, wos_ref[...]) + _dot(o_m, wom_ref[...]) + _dot(o_g, wog_ref[...])
    xo_ref[...] = x_new
    h = _rms(x_new, fn_ref[...])
    h_hi, h_lo = _split_bf16(h)
    h_ref[...] = h_hi
    logits = _dot(h_hi, wrh_ref[...]) + _dot(h_lo, wrh_ref[...]) + _dot(h_hi, wrl_ref[...]) + rb_ref[...]
    lt = logits.T
    row = lax.broadcasted_iota(jnp.int32, (8, lt.shape[1]), 0)

    def first_argmax(v, vmax):
        return jnp.min(jnp.where(v == vmax, row, 8), axis=0, keepdims=True)

    gl = lt[0:8]
    gmax = jnp.max(gl, axis=0, keepdims=True)
    g_w = 1.0 / jnp.sum(jnp.exp(gl - gmax), axis=0, keepdims=True)
    g_idx = first_argmax(gl, gmax)
    el = lt[8:16]
    for g in range(1, N_EXPERT_GROUPS):
        el = jnp.where(g_idx == g, lt[8 + 8 * g:16 + 8 * g], el)
    ee = jnp.exp(el - jnp.max(el, axis=0, keepdims=True))
    ep = ee / jnp.sum(ee, axis=0, keepdims=True)
    p1 = jnp.max(ep, axis=0, keepdims=True)
    i1 = first_argmax(ep, p1)
    ep2 = jnp.where(row == i1, -1.0, ep)
    p2 = jnp.max(ep2, axis=0, keepdims=True)
    i2 = first_argmax(ep2, p2)
    denom = p1 + p2
    idx_ref[0:1, :] = g_idx * EXPERTS_PER_GROUP + i1
    idx_ref[1:2, :] = g_idx * EXPERTS_PER_GROUP + i2
    wt_ref[0:1, :] = g_w * p1 / denom
    wt_ref[1:2, :] = g_w * p2 / denom


def _outproj(y, o_m, o_g, x, lw):
    n_tok = x.shape[0]
    tm = min(512, n_tok)

    def row(i):
        return (i, 0)

    def full(i):
        return (0, 0)

    consts = [lw['w_glu'], lw['b_glu'], lw['out_norm_ssm'], lw['out_norm_mla'], lw['out_norm_gqa'],
              lw['w_out_s'], lw['w_out_m'], lw['w_out_g'], lw['ffn_norm'],
              lw['w_r_hi'], lw['w_r_lo'], lw['r_bias']]
    in_specs = [pl.BlockSpec((tm, SSM_WIDTH), row), pl.BlockSpec((tm, MLA_WIDTH), row),
                pl.BlockSpec((tm, GQA_WIDTH), row), pl.BlockSpec((tm, D_MODEL), row)]
    in_specs += [pl.BlockSpec(c.shape, full) for c in consts]
    return pl.pallas_call(
        _outproj_kernel,
        grid=(n_tok // tm,),
        in_specs=in_specs,
        out_specs=[pl.BlockSpec((tm, D_MODEL), row), pl.BlockSpec((tm, D_MODEL), row),
                   pl.BlockSpec((2, tm), lambda i: (0, i)), pl.BlockSpec((2, tm), lambda i: (0, i))],
        out_shape=[jax.ShapeDtypeStruct((n_tok, D_MODEL), F32), jax.ShapeDtypeStruct((n_tok, D_MODEL), BF16),
                   jax.ShapeDtypeStruct((2, n_tok), jnp.int32), jax.ShapeDtypeStruct((2, n_tok), F32)],
        compiler_params=_cparams("parallel"),
        name="outproj_router",
    )(y, o_m, o_g, x, *consts)


def _expert_kernel(be_ref, nu_ref, x_ref, wg_ref, wu_ref, wd_ref, o_ref):
    @pl.when(pl.program_id(0) < nu_ref[0])
    def _():
        xb = x_ref[...]
        hg = _dot(xb, wg_ref[0])
        hu = _dot(xb, wu_ref[0])
        act = (hg * jax.nn.sigmoid(hg) * hu).astype(BF16)
        o_ref[...] = _dot(act, wd_ref[0]).astype(BF16)

    @pl.when(pl.program_id(0) >= nu_ref[0])
    def _():
        o_ref[...] = jnp.zeros_like(o_ref)


def _experts(buf, block_expert, n_used, w_gate, w_up, w_down):
    rows = buf.shape[0]
    n_blocks = rows // MOE_TILE
    grid_spec = pltpu.PrefetchScalarGridSpec(
        num_scalar_prefetch=2,
        grid=(n_blocks,),
        in_specs=[pl.BlockSpec((MOE_TILE, D_MODEL), lambda i, be, nu: (i, 0)),
                  pl.BlockSpec((1, D_MODEL, EXPERT_FF), lambda i, be, nu: (be[i], 0, 0)),
                  pl.BlockSpec((1, D_MODEL, EXPERT_FF), lambda i, be, nu: (be[i], 0, 0)),
                  pl.BlockSpec((1, EXPERT_FF, D_MODEL), lambda i, be, nu: (be[i], 0, 0))],
        out_specs=pl.BlockSpec((MOE_TILE, D_MODEL), lambda i, be, nu: (i, 0)),
    )
    return pl.pallas_call(
        _expert_kernel,
        grid_spec=grid_spec,
        out_shape=jax.ShapeDtypeStruct((rows, D_MODEL), BF16),
        compiler_params=_cparams("arbitrary"),
        name="experts",
    )(block_expert, n_used, buf, w_gate, w_up, w_down)


def _combine_kernel(x_ref, g0_ref, g1_ref, w_ref, gain_ref, o_ref, *, final):
    w = w_ref[...]
    y = x_ref[...] + w[:, 0:1] * g0_ref[...].astype(F32) + w[:, 1:2] * g1_ref[...].astype(F32)
    if final:
        y = _rms(y, gain_ref[...])
    o_ref[...] = y


def _combine(x, g0, g1, w_col, gain, final):
    n_tok = x.shape[0]
    tm = min(512, n_tok)

    def row(i):
        return (i, 0)

    return pl.pallas_call(
        functools.partial(_combine_kernel, final=final),
        grid=(n_tok // tm,),
        in_specs=[pl.BlockSpec((tm, D_MODEL), row), pl.BlockSpec((tm, D_MODEL), row),
                  pl.BlockSpec((tm, D_MODEL), row), pl.BlockSpec((tm, 2), row),
                  pl.BlockSpec((1, D_MODEL), lambda i: (0, 0))],
        out_specs=pl.BlockSpec((tm, D_MODEL), row),
        out_shape=jax.ShapeDtypeStruct((n_tok, D_MODEL), F32),
        compiler_params=_cparams("parallel"),
        name="moe_combine",
    )(x, g0, g1, w_col, gain)


def _moe(h, idx, wts, x, lw, gain, final):
    n_tok = h.shape[0]
    n_assign = 2 * n_tok
    flat_e = idx.reshape(-1)
    order = jnp.argsort(flat_e)
    sorted_e = flat_e[order]
    counts = jnp.bincount(flat_e, length=N_EXPERTS)
    padded = (counts + MOE_TILE - 1) // MOE_TILE * MOE_TILE
    pad_end = jnp.cumsum(padded)
    pad_start = pad_end - padded
    start = jnp.cumsum(counts) - counts
    dest = pad_start[sorted_e] + jnp.arange(n_assign, dtype=jnp.int32) - start[sorted_e]
    n_blocks = -(-n_assign // MOE_TILE) + N_EXPERTS
    rows = n_blocks * MOE_TILE
    src_tok = jnp.zeros((rows,), jnp.int32).at[dest].set((order % n_tok).astype(jnp.int32))
    pos = jnp.zeros((n_assign,), jnp.int32).at[order].set(dest.astype(jnp.int32))
    block_expert = jnp.minimum(
        jnp.searchsorted(pad_end, jnp.arange(n_blocks) * MOE_TILE, side='right'), N_EXPERTS - 1).astype(jnp.int32)
    n_used = (pad_end[-1] // MOE_TILE).astype(jnp.int32).reshape(1)
    buf = jnp.take(h, src_tok, axis=0)
    out = _experts(buf, block_expert, n_used, lw['w_gate'], lw['w_up'], lw['w_down'])
    g0 = jnp.take(out, pos[:n_tok], axis=0)
    g1 = jnp.take(out, pos[n_tok:], axis=0)
    return _combine(x, g0, g1, wts.T, gain, final)


def _rope_tables(seq_len):
    t = jnp.arange(seq_len)
    row_idx = (t // GRID_W).astype(F32)
    col_idx = (t % GRID_W).astype(F32)

    def cos_sin(rot_dim):
        n_freq = rot_dim // 4
        inv_freq = ROPE_THETA ** (-jnp.arange(n_freq, dtype=F32) / n_freq)
        ang = jnp.concatenate([row_idx[:, None] * inv_freq, col_idx[:, None] * inv_freq], axis=-1)
        return jnp.cos(ang), jnp.sin(ang)

    cm, sm = cos_sin(MLA_ROPE)
    cg, sg = cos_sin(GQA_HEAD_DIM)
    one = jnp.ones((seq_len, MLA_NOPE), F32)
    zero = jnp.zeros((seq_len, MLA_NOPE), F32)
    pad = jnp.zeros((seq_len, SLOT - MLA_NOPE - MLA_ROPE), F32)
    mla_c = jnp.concatenate([one, cm, cm, pad], axis=-1)
    mla_s = jnp.concatenate([zero, sm, sm, pad], axis=-1)
    gqa_c = jnp.concatenate([cg, cg, cg, cg], axis=-1)
    gqa_s = jnp.concatenate([-sg, sg, -sg, sg], axis=-1)
    return mla_c, mla_s, gqa_c, gqa_s


def _s5_weights(lam_re, lam_im, log_dt, b_re, b_im, c_re, c_im, d_skip):
    t_len = S5_CHUNK
    lam = lax.complex(lam_re.astype(F32), lam_im.astype(F32))
    dt = jnp.exp(log_dt.astype(F32))[..., None]
    lam_dt = lam * dt
    lam_bar = jnp.exp(lam_dt)
    b_bar = ((lam_bar - 1.0) / lam)[..., None] * lax.complex(b_re.astype(F32), b_im.astype(F32))
    c_mat = lax.complex(c_re.astype(F32), c_im.astype(F32))
    k_idx = jnp.arange(t_len + 1, dtype=F32)
    pw = jnp.exp(lam_dt[:, None] * k_idx[None, :, None, None])
    kern = jnp.einsum('dgcp,dkgp,dgpe->dkgce', c_mat, pw[:, :t_len], b_bar).real
    d_eye = jnp.eye(SSM_GROUP, dtype=F32) * d_skip.astype(F32).reshape(SSM_GROUPS, SSM_GROUP, 1)
    t_i = jnp.arange(t_len)
    lag = t_i[None, :] - t_i[:, None]
    kf = kern[0][jnp.clip(lag, 0, t_len - 1)]
    kb = kern[1][jnp.clip(-lag, 0, t_len - 1)]
    lag5 = lag[:, :, None, None, None]
    toep = jnp.where(lag5 > 0, kf, 0.0) + jnp.where(lag5 < 0, kb, 0.0) \
        + jnp.where(lag5 == 0, kf + kb + d_eye[None, None], 0.0)
    toep = toep.transpose(2, 0, 4, 1, 3).reshape(SSM_GROUPS, S5_K, S5_K)
    wf = pw[0][t_len - 1 - t_i][:, :, :, None] * b_bar[0][None]
    wb = pw[1][t_i][:, :, :, None] * b_bar[1][None]

    def state_cols(w):
        w = w.transpose(1, 0, 3, 2).reshape(SSM_GROUPS, S5_K, SSM_STATE)
        return jnp.concatenate([w.real, w.imag, w.imag, w.real], axis=-1)

    w_s = jnp.concatenate([state_cols(wf), state_cols(wb)], axis=-1)
    yf = c_mat[0][None] * pw[0][t_i + 1][:, :, None, :]
    yb = c_mat[1][None] * pw[1][t_len - t_i][:, :, None, :]

    def out_rows(w):
        w = w.transpose(1, 3, 0, 2).reshape(SSM_GROUPS, SSM_STATE, S5_K)
        return jnp.concatenate([w.real, -w.imag], axis=1)

    w_y = jnp.concatenate([out_rows(yf), out_rows(yb)], axis=1)
    a = pw[:, t_len]
    ar, ai = a.real, a.imag

    def coef(d):
        return jnp.stack([jnp.concatenate([ar[d], ar[d]], -1), jnp.concatenate([-ai[d], ai[d]], -1),
                          jnp.concatenate([ai[d], -ai[d]], -1)], axis=1)

    s5_coef = jnp.concatenate([coef(0), coef(1)], axis=-1)
    return toep.astype(BF16), w_s.astype(BF16), w_y.astype(BF16), s5_coef


def _static_mats():
    bd = np.kron(np.eye(GQA_HEADS), np.full((GQA_HEAD_DIM, GQA_HEAD_DIM), 1.0 / GQA_HEAD_DIM))
    half = GQA_HEAD_DIM // 2
    swap = np.zeros((GQA_HEAD_DIM, GQA_HEAD_DIM))
    swap[np.arange(half) + half, np.arange(half)] = 1.0
    swap[np.arange(half), np.arange(half) + half] = 1.0
    perm = np.kron(np.eye(GQA_HEADS), swap)
    place = np.zeros((GQA_WIDTH, GQA_HEADS * SLOT))
    for hd in range(GQA_HEADS):
        kv = hd // (GQA_HEADS // GQA_KV_HEADS)
        for d in range(GQA_HEAD_DIM):
            place[hd * GQA_HEAD_DIM + d, hd * SLOT + kv * GQA_HEAD_DIM + d] = 1.0
    return (jnp.asarray(bd, BF16), jnp.asarray(perm, BF16), jnp.asarray(place, BF16))


def _layer_weights(l, p):
    f = lambda name: p[name][l].astype(F32)
    lw = {}
    w_in = f('w_in')
    u, c_q, c_kv, k_pe, q_g, k_g, v_g = jnp.split(w_in, [256, 512, 640, 672, 1056, 1184], axis=-1)
    z = lambda n: jnp.zeros((D_MODEL, n), F32)
    half = MLA_ROPE // 2
    kpe_slot = jnp.concatenate([z(MLA_NOPE), k_pe, z(32)], axis=-1)
    kpe_rot = jnp.concatenate([z(MLA_NOPE), -k_pe[:, half:], k_pe[:, :half], z(32)], axis=-1)
    lw['w_in'] = jnp.concatenate([u, c_q, c_kv, kpe_slot, kpe_rot, q_g, k_g, v_g], axis=-1).astype(BF16)
    lw['attn_norm'] = f('attn_norm')[None]
    lw['mla_q_norm'] = f('mla_q_norm')[None]
    lw['mla_kv_norm'] = f('mla_kv_norm')[None]
    w_uq = f('mla_w_uq').reshape(MLA_Q_LORA, MLA_HEADS, MLA_NOPE + MLA_ROPE)
    nope, pe = w_uq[..., :MLA_NOPE], w_uq[..., MLA_NOPE:]
    zq = jnp.zeros((MLA_Q_LORA, MLA_HEADS, 32), F32)
    lw['w_uq'] = jnp.concatenate([nope, pe, zq], axis=-1).reshape(MLA_Q_LORA, -1).astype(BF16)
    lw['w_uq_rot'] = jnp.concatenate([jnp.zeros_like(nope), -pe[..., half:], pe[..., :half], zq],
                                     axis=-1).reshape(MLA_Q_LORA, -1).astype(BF16)
    w_ukv = f('mla_w_ukv').reshape(MLA_KV_LORA, MLA_HEADS, MLA_NOPE + MLA_V)
    zk = jnp.zeros((MLA_KV_LORA, MLA_HEADS, SLOT - MLA_NOPE), F32)
    lw['w_ukv_k'] = jnp.concatenate([w_ukv[..., :MLA_NOPE], zk], axis=-1).reshape(MLA_KV_LORA, -1).astype(BF16)
    lw['w_ukv_v'] = w_ukv[..., MLA_NOPE:].reshape(MLA_KV_LORA, -1).astype(BF16)
    lw['gqa_q_norm'] = jnp.tile(f('gqa_q_norm'), GQA_HEADS)[None]
    lw['gqa_k_norm'] = jnp.tile(f('gqa_k_norm'), GQA_KV_HEADS)[None]
    lw['bd'], lw['perm'], lw['place'] = _static_mats()
    lw['s5_toep'], lw['s5_w_s'], lw['s5_w_y'], lw['s5_coef'] = _s5_weights(
        p['s5_lam_re'][l], p['s5_lam_im'][l], p['s5_log_dt'][l], p['s5_b_re'][l], p['s5_b_im'][l],
        p['s5_c_re'][l], p['s5_c_im'][l], p['s5_d'][l])
    lw['w_glu'] = f('s5_w_glu').astype(BF16)
    lw['b_glu'] = f('s5_b_glu')[None]
    lw['out_norm_ssm'] = f('out_norm_ssm')[None]
    lw['out_norm_mla'] = f('out_norm_mla')[None]
    lw['out_norm_gqa'] = f('out_norm_gqa')[None]
    w_out = f('w_out')
    lw['w_out_s'] = w_out[:SSM_WIDTH].astype(BF16)
    lw['w_out_m'] = w_out[SSM_WIDTH:SSM_WIDTH + MLA_WIDTH].astype(BF16)
    lw['w_out_g'] = w_out[SSM_WIDTH + MLA_WIDTH:].astype(BF16)
    lw['ffn_norm'] = f('ffn_norm')[None]
    w_r = jnp.concatenate([f('router_group_w'), jnp.zeros((D_MODEL, 4), F32), f('router_expert_w'),
                           jnp.zeros((D_MODEL, LANE - 8 - N_EXPERTS), F32)], axis=-1)
    lw['w_r_hi'], lw['w_r_lo'] = _split_bf16(w_r)
    lw['r_bias'] = jnp.concatenate([f('router_group_b'), jnp.full((4,), -1e30, F32), f('router_expert_b'),
                                    jnp.zeros((LANE - 8 - N_EXPERTS,), F32)])[None]
    lw['w_gate'] = p['expert_w_gate'][l].astype(BF16)
    lw['w_up'] = p['expert_w_up'][l].astype(BF16)
    lw['w_down'] = p['expert_w_down'][l].astype(BF16)
    return lw


def _trunk(x, layers, final_gain):
    bsz, seq_len, _ = x.shape
    tabs = _rope_tables(seq_len)
    xf = x.reshape(bsz * seq_len, D_MODEL).astype(F32)
    depth = len(layers)
    for l, lw in enumerate(layers):
        u, q_m, k_m, v_m, q_g, k_g, v_g = _inproj(xf, lw, tabs, seq_len)
        y = _s5(u, lw, bsz, seq_len)
        o_m = _attention(q_m, k_m, v_m, bsz, seq_len, groups=MLA_HEADS // 2, heads=2, kslots=2,
                         kslot=(0, 1), voff=(0, 64))
        o_g = _attention(q_g, k_g, v_g, bsz, seq_len, groups=1, heads=GQA_HEADS, kslots=1,
                         kslot=(0,) * GQA_HEADS, voff=(0, 0, 0, 64, 64, 64))
        x_mid, h, idx, wts = _outproj(y, o_m, o_g, xf, lw)
        xf = _moe(h, idx, wts, x_mid, lw, final_gain, final=(l == depth - 1))
    return xf.reshape(bsz, seq_len, D_MODEL)


def kernel(x_prompt, x_sample, attn_norm, w_in, s5_lam_re, s5_lam_im, s5_log_dt, s5_b_re, s5_b_im, s5_c_re, s5_c_im, s5_d, s5_w_glu, s5_b_glu, mla_q_norm, mla_w_uq, mla_kv_norm, mla_w_ukv, gqa_q_norm, gqa_k_norm, out_norm_ssm, out_norm_mla, out_norm_gqa, w_out, ffn_norm, router_group_w, router_group_b, router_expert_w, router_expert_b, expert_w_gate, expert_w_up, expert_w_down, final_norm):
    p = dict(attn_norm=attn_norm, w_in=w_in, s5_lam_re=s5_lam_re, s5_lam_im=s5_lam_im, s5_log_dt=s5_log_dt,
             s5_b_re=s5_b_re, s5_b_im=s5_b_im, s5_c_re=s5_c_re, s5_c_im=s5_c_im, s5_d=s5_d,
             s5_w_glu=s5_w_glu, s5_b_glu=s5_b_glu, mla_q_norm=mla_q_norm, mla_w_uq=mla_w_uq,
             mla_kv_norm=mla_kv_norm, mla_w_ukv=mla_w_ukv, gqa_q_norm=gqa_q_norm, gqa_k_norm=gqa_k_norm,
             out_norm_ssm=out_norm_ssm, out_norm_mla=out_norm_mla, out_norm_gqa=out_norm_gqa, w_out=w_out,
             ffn_norm=ffn_norm, router_group_w=router_group_w, router_group_b=router_group_b,
             router_expert_w=router_expert_w, router_expert_b=router_expert_b,
             expert_w_gate=expert_w_gate, expert_w_up=expert_w_up, expert_w_down=expert_w_down)
    depth = w_in.shape[0]
    layers = [_layer_weights(l, p) for l in range(depth)]
    final_gain = final_norm.astype(F32)[None]
    return (_trunk(x_prompt, layers, final_gain), _trunk(x_sample, layers, final_gain))
```

```python
import functools
import math

import jax
import jax.numpy as jnp
import numpy as np
from jax import lax
from jax.experimental import pallas as pl
from jax.experimental.pallas import tpu as pltpu

F32 = jnp.float32
BF16 = jnp.bfloat16

D_MODEL = 1024
GRID_W = 64
NORM_EPS = 1e-6
ROPE_THETA = 10000.0

SSM_WIDTH = 256
SSM_GROUP = 16
SSM_GROUPS = 16
SSM_STATE = 64

MLA_HEADS = 6
MLA_NOPE = 64
MLA_ROPE = 32
MLA_V = 64
MLA_Q_LORA = 256
MLA_KV_LORA = 128
MLA_WIDTH = MLA_HEADS * MLA_V

GQA_HEADS = 6
GQA_KV_HEADS = 2
GQA_HEAD_DIM = 64
GQA_WIDTH = GQA_HEADS * GQA_HEAD_DIM

N_EXPERT_GROUPS = 4
EXPERTS_PER_GROUP = 8
N_EXPERTS = 32
EXPERT_FF = 512

LANE = 128
SLOT = 128
S5_CHUNK = 64
S5_K = S5_CHUNK * SSM_GROUP
S5_SW = 512
S5_HW = 256
WIN_PAD = 1536
MOE_TILE = 256
VMEM_LIMIT = 48 * 1024 * 1024
LOG2E = math.log2(math.e)

_C_U, _C_CQ, _C_CKV, _C_KPE, _C_KPER, _C_QG, _C_KG, _C_VG = 0, 256, 512, 640, 768, 896, 1280, 1408


def _cparams(*sem):
    return pltpu.CompilerParams(dimension_semantics=sem, vmem_limit_bytes=VMEM_LIMIT)


def _split_bf16(x):
    hi = x.astype(BF16)
    lo = (x - hi.astype(F32)).astype(BF16)
    return hi, lo


def _dot(a, b):
    return jnp.dot(a, b, preferred_element_type=F32)


def _dot2(x, w):
    hi, lo = _split_bf16(x)
    return _dot(hi, w) + _dot(lo, w)


def _rms(x, gain):
    return x * lax.rsqrt(jnp.mean(x * x, axis=-1, keepdims=True) + NORM_EPS) * gain


def _inproj_kernel(x_ref, g_ref, win_ref, qng_ref, wuq_ref, wuqr_ref, kvng_ref, wukk_ref, wukv_ref,
                   gqn_ref, gkn_ref, bd_ref, perm_ref, place_ref,
                   mc_ref, ms_ref, gc_ref, gs_ref,
                   u_ref, qm_ref, km_ref, vm_ref, qg_ref, kg_ref, vg_ref):
    h = _rms(x_ref[...], g_ref[...]).astype(BF16)
    proj = _dot(h, win_ref[...])
    u_ref[...] = proj[:, _C_U:_C_U + SSM_WIDTH].astype(BF16)
    vg_ref[...] = proj[:, _C_VG:_C_VG + 128].astype(BF16)

    mc = mc_ref[...]
    ms = ms_ref[...]
    cq = _rms(proj[:, _C_CQ:_C_CQ + MLA_Q_LORA], qng_ref[...]).astype(BF16)
    qa = _dot(cq, wuq_ref[...])
    qb = _dot(cq, wuqr_ref[...])
    scale = (MLA_NOPE + MLA_ROPE) ** -0.5 * LOG2E
    for hd in range(MLA_HEADS):
        sl = slice(hd * SLOT, (hd + 1) * SLOT)
        qm_ref[:, sl] = ((qa[:, sl] * mc + qb[:, sl] * ms) * scale).astype(BF16)
    ckv = _rms(proj[:, _C_CKV:_C_CKV + MLA_KV_LORA], kvng_ref[...]).astype(BF16)
    kk = _dot(ckv, wukk_ref[...])
    kpe = proj[:, _C_KPE:_C_KPE + SLOT] * mc + proj[:, _C_KPER:_C_KPER + SLOT] * ms
    for hd in range(MLA_HEADS):
        sl = slice(hd * SLOT, (hd + 1) * SLOT)
        km_ref[:, sl] = (kk[:, sl] + kpe).astype(BF16)
    vm_ref[...] = _dot(ckv, wukv_ref[...]).astype(BF16)

    gc = gc_ref[...]
    gs = gs_ref[...]
    bd = bd_ref[...]
    perm = perm_ref[...]
    qg = proj[:, _C_QG:_C_QG + GQA_WIDTH]
    qn = qg * lax.rsqrt(_dot2(qg * qg, bd) + NORM_EPS) * gqn_ref[...]
    gc3 = jnp.concatenate([gc, gc, gc], axis=-1)
    gs3 = jnp.concatenate([gs, gs, gs], axis=-1)
    qr = (qn * gc3 + _dot2(qn, perm) * gs3) * (GQA_HEAD_DIM ** -0.5 * LOG2E)
    qg_ref[...] = _dot(qr.astype(BF16), place_ref[...]).astype(BF16)
    kg = proj[:, _C_KG:_C_KG + 128]
    kn = kg * lax.rsqrt(_dot2(kg * kg, bd[:128, :128]) + NORM_EPS) * gkn_ref[...]
    kg_ref[...] = (kn * gc + _dot2(kn, perm[:128, :128]) * gs).astype(BF16)


def _inproj(x, lw, tabs, seq_len):
    n_tok = x.shape[0]
    tm = min(512, seq_len)
    nt = n_tok // tm
    per_seq = seq_len // tm

    def row(i):
        return (i, 0)

    def full(i):
        return (0, 0)

    def tab(i):
        return (i % per_seq, 0)

    consts = [lw['attn_norm'], lw['w_in'], lw['mla_q_norm'], lw['w_uq'], lw['w_uq_rot'],
              lw['mla_kv_norm'], lw['w_ukv_k'], lw['w_ukv_v'], lw['gqa_q_norm'], lw['gqa_k_norm'],
              lw['bd'], lw['perm'], lw['place']]
    in_specs = [pl.BlockSpec((tm, D_MODEL), row)]
    in_specs += [pl.BlockSpec(c.shape, full) for c in consts]
    in_specs += [pl.BlockSpec((tm, LANE), tab)] * 4
    widths = [SSM_WIDTH, MLA_HEADS * SLOT, MLA_HEADS * SLOT, MLA_WIDTH, GQA_HEADS * SLOT, 128, 128]
    return pl.pallas_call(
        _inproj_kernel,
        grid=(nt,),
        in_specs=in_specs,
        out_specs=[pl.BlockSpec((tm, w), row) for w in widths],
        out_shape=[jax.ShapeDtypeStruct((n_tok, w), BF16) for w in widths],
        compiler_params=_cparams("parallel"),
        name="inproj",
    )(x, *consts, *tabs)


def _s5_state_kernel(u_ref, w_ref, s_ref):
    s_ref[...] = _dot(u_ref[0], w_ref[0])


def _s5_state(u_t, w_s):
    n_grp, rows, _ = u_t.shape
    tr = min(512, rows)
    return pl.pallas_call(
        _s5_state_kernel,
        grid=(n_grp, rows // tr),
        in_specs=[pl.BlockSpec((1, tr, S5_K), lambda g, r: (g, r, 0)),
                  pl.BlockSpec((1, S5_K, S5_SW), lambda g, r: (g, 0, 0))],
        out_specs=pl.BlockSpec((tr, S5_SW), lambda g, r: (r, g)),
        out_shape=jax.ShapeDtypeStruct((rows, n_grp * S5_SW), F32),
        compiler_params=_cparams("parallel", "parallel"),
        name="s5_state",
    )(u_t, w_s)


def _s5_carry_kernel(s_ref, c_ref, h_ref, *, n_chunks):
    a1f, a2f, a3f = c_ref[0, 0:1, 0:LANE], c_ref[0, 1:2, 0:LANE], c_ref[0, 2:3, 0:LANE]
    a1b, a2b, a3b = c_ref[0, 0:1, LANE:], c_ref[0, 1:2, LANE:], c_ref[0, 2:3, LANE:]
    bsz = s_ref.shape[1]
    zero = jnp.zeros((bsz, LANE), F32)

    def body(i, carry):
        hf, hfs, hb, hbs = carry
        jb = n_chunks - 1 - i
        h_ref[i, :, 0:LANE] = hf
        h_ref[jb, :, LANE:2 * LANE] = hb
        sf = s_ref[i, :, 0:LANE]
        sfs = s_ref[i, :, LANE:2 * LANE]
        sb = s_ref[jb, :, 2 * LANE:3 * LANE]
        sbs = s_ref[jb, :, 3 * LANE:4 * LANE]
        return (a1f * hf + a2f * hfs + sf, a1f * hfs + a3f * hf + sfs,
                a1b * hb + a2b * hbs + sb, a1b * hbs + a3b * hb + sbs)

    lax.fori_loop(0, n_chunks, body, (zero, zero, zero, zero))


def _s5_carry(s, coef, n_chunks, bsz):
    n_grp = coef.shape[0]
    s3 = s.reshape(n_chunks, bsz, n_grp * S5_SW)
    h = pl.pallas_call(
        functools.partial(_s5_carry_kernel, n_chunks=n_chunks),
        grid=(n_grp,),
        in_specs=[pl.BlockSpec((n_chunks, bsz, S5_SW), lambda g: (0, 0, g)),
                  pl.BlockSpec((1, 3, S5_HW), lambda g: (g, 0, 0))],
        out_specs=pl.BlockSpec((n_chunks, bsz, S5_HW), lambda g: (0, 0, g)),
        out_shape=jax.ShapeDtypeStruct((n_chunks, bsz, n_grp * S5_HW), F32),
        compiler_params=_cparams("parallel"),
        name="s5_carry",
    )(s3, coef)
    return h.reshape(n_chunks * bsz, n_grp * S5_HW)


def _gelu_tanh(y):
    return 0.5 * y * (1.0 + jnp.tanh(math.sqrt(2.0 / math.pi) * (y + 0.044715 * (y * y * y))))


def _s5_out_kernel(u_ref, h_ref, toep_ref, wy_ref, y_ref):
    y = _dot(u_ref[0], toep_ref[0]) + _dot(h_ref[...].astype(BF16), wy_ref[0])
    y_ref[0] = _gelu_tanh(y)


def _s5_out(u_t, h_in, toep, w_y):
    n_grp, rows, _ = u_t.shape
    tr = min(512, rows)
    return pl.pallas_call(
        _s5_out_kernel,
        grid=(n_grp, rows // tr),
        in_specs=[pl.BlockSpec((1, tr, S5_K), lambda g, r: (g, r, 0)),
                  pl.BlockSpec((tr, S5_HW), lambda g, r: (r, g)),
                  pl.BlockSpec((1, S5_K, S5_K), lambda g, r: (g, 0, 0)),
                  pl.BlockSpec((1, S5_HW, S5_K), lambda g, r: (g, 0, 0))],
        out_specs=pl.BlockSpec((1, tr, S5_K), lambda g, r: (g, r, 0)),
        out_shape=jax.ShapeDtypeStruct((n_grp, rows, S5_K), F32),
        compiler_params=_cparams("parallel", "parallel"),
        name="s5_out",
    )(u_t, h_in, toep, w_y)


def _s5(u, lw, bsz, seq_len):
    n_chunks = seq_len // S5_CHUNK
    u_t = u.reshape(bsz, n_chunks, S5_CHUNK, SSM_GROUPS, SSM_GROUP)
    u_t = u_t.transpose(3, 1, 0, 2, 4).reshape(SSM_GROUPS, n_chunks * bsz, S5_K)
    s = _s5_state(u_t, lw['s5_w_s'])
    h_in = _s5_carry(s, lw['s5_coef'], n_chunks, bsz)
    y_t = _s5_out(u_t, h_in, lw['s5_toep'], lw['s5_w_y'])
    y = y_t.reshape(SSM_GROUPS, n_chunks, bsz, S5_CHUNK, SSM_GROUP).transpose(2, 1, 3, 0, 4)
    return y.reshape(bsz * seq_len, SSM_WIDTH)


def _attn_kernel(q_ref, k_ref, vt_ref, o_ref, m_ref, a_ref, acc_ref, s_ref, *, qslots, kslot, tk, n_kv):
    tq = q_ref.shape[0]
    n_virt = len(qslots)
    m_ref[...] = jnp.full(m_ref.shape, -jnp.inf, F32)
    acc_ref[...] = jnp.zeros(acc_ref.shape, F32)

    def scores(j, i):
        q = jnp.concatenate([q_ref[:, s * SLOT:(s + 1) * SLOT] for s in qslots[j]], axis=0)
        start = pl.multiple_of(i * tk, tk)
        kb = k_ref[pl.ds(start, tk), kslot[j] * SLOT:(kslot[j] + 1) * SLOT]
        st = lax.dot_general(kb, q, (((1,), (1,)), ((), ())), preferred_element_type=F32)
        s_ref[j] = st
        m_old = m_ref[j]
        m_new = jnp.maximum(m_old, jnp.max(st, axis=0, keepdims=True))
        m_ref[j] = m_new
        a_ref[j] = jnp.exp2(m_old - m_new)

    def values(j, i):
        pt = jnp.exp2(s_ref[j] - m_ref[j]).astype(BF16)
        acc_ref[j] = a_ref[j] * acc_ref[j] + _dot(vt_ref[j, i], pt)

    scores(0, 0)

    def body(i, carry):
        scores(1, i)
        values(0, i)
        scores(0, i + 1)
        values(1, i)
        return carry

    lax.fori_loop(0, n_kv - 1, body, 0, unroll=2)
    scores(1, n_kv - 1)
    values(0, n_kv - 1)
    values(1, n_kv - 1)
    pieces = []
    for j in range(n_virt):
        acc = acc_ref[j]
        o = acc[0:64] / acc[64:65]
        for n in range(len(qslots[j])):
            pieces.append(o[:, n * tq:(n + 1) * tq])
    for a in range(len(pieces) // 2):
        pair = jnp.concatenate([pieces[2 * a], pieces[2 * a + 1]], axis=0)
        o_ref[:, a * SLOT:(a + 1) * SLOT] = pair.T


V_ROWS = 80


def _attention(q, k, v, bsz, seq_len, *, groups, qslots, kslots, kslot, tq):
    tq = min(tq, seq_len)
    tk = min(512, seq_len)
    nq = seq_len // tq
    n_kv = seq_len // tk
    heads = sum(len(s) for s in qslots)
    n_stack = len(qslots[0])
    q3 = q.reshape(bsz, seq_len, -1)
    k3 = k.reshape(bsz, seq_len, -1)
    vt = v.reshape(bsz, n_kv, tk, groups * 2, 64).transpose(0, 3, 1, 4, 2)
    vt = jnp.concatenate([vt, jnp.ones((bsz, groups * 2, n_kv, V_ROWS - 64, tk), BF16)], axis=3)
    ow = heads * 64
    out = pl.pallas_call(
        functools.partial(_attn_kernel, qslots=qslots, kslot=kslot, tk=tk, n_kv=n_kv),
        grid=(bsz, groups, nq),
        in_specs=[pl.BlockSpec((None, tq, heads * SLOT), lambda b, g, i: (b, i, g)),
                  pl.BlockSpec((None, seq_len, kslots * SLOT), lambda b, g, i: (b, 0, g)),
                  pl.BlockSpec((None, 2, n_kv, V_ROWS, tk), lambda b, g, i: (b, g, 0, 0, 0))],
        out_specs=pl.BlockSpec((None, tq, ow), lambda b, g, i: (b, i, g)),
        out_shape=jax.ShapeDtypeStruct((bsz, seq_len, groups * ow), F32),
        scratch_shapes=[pltpu.VMEM((2, 1, n_stack * tq), F32), pltpu.VMEM((2, 1, n_stack * tq), F32),
                        pltpu.VMEM((2, V_ROWS, n_stack * tq), F32), pltpu.VMEM((2, tk, n_stack * tq), F32)],
        compiler_params=_cparams("parallel", "parallel", "arbitrary"),
        name="attention",
    )(q3, k3, vt)
    return out.reshape(bsz * seq_len, groups * ow)


def _outproj_kernel(y_ref, om_ref, og_ref, x_ref, wglu_ref, bglu_ref, ns_ref, nm_ref, ng_ref,
                    wos_ref, wom_ref, wog_ref, fn_ref, wrh_ref, wrl_ref, rb_ref,
                    xo_ref, h_ref, idx_ref, wt_ref):
    y = y_ref[...]
    gate = jax.nn.sigmoid(_dot(y.astype(BF16), wglu_ref[...]) + bglu_ref[...])
    o_s = _rms(y * gate, ns_ref[...]).astype(BF16)
    o_m = _rms(om_ref[...], nm_ref[...]).astype(BF16)
    o_g = _rms(og_ref[...], ng_ref[...]).astype(BF16)
    x_new = x_ref[...] + _dot(o_s, wos_ref[...]) + _dot(o_m, wom_ref[...]) + _dot(o_g, wog_ref[...])
    xo_ref[...] = x_new
    h = _rms(x_new, fn_ref[...])
    h_hi, h_lo = _split_bf16(h)
    h_ref[...] = h_hi
    logits = _dot(h_hi, wrh_ref[...]) + _dot(h_lo, wrh_ref[...]) + _dot(h_hi, wrl_ref[...]) + rb_ref[...]
    lt = logits.T
    row = lax.broadcasted_iota(jnp.int32, (8, lt.shape[1]), 0)

    def first_argmax(v, vmax):
        return jnp.min(jnp.where(v == vmax, row, 8), axis=0, keepdims=True)

    gl = lt[0:8]
    gmax = jnp.max(gl, axis=0, keepdims=True)
    g_w = 1.0 / jnp.sum(jnp.exp(gl - gmax), axis=0, keepdims=True)
    g_idx = first_argmax(gl, gmax)
    el = lt[8:16]
    for g in range(1, N_EXPERT_GROUPS):
        el = jnp.where(g_idx == g, lt[8 + 8 * g:16 + 8 * g], el)
    ee = jnp.exp(el - jnp.max(el, axis=0, keepdims=True))
    ep = ee / jnp.sum(ee, axis=0, keepdims=True)
    p1 = jnp.max(ep, axis=0, keepdims=True)
    i1 = first_argmax(ep, p1)
    ep2 = jnp.where(row == i1, -1.0, ep)
    p2 = jnp.max(ep2, axis=0, keepdims=True)
    i2 = first_argmax(ep2, p2)
    denom = p1 + p2
    idx_ref[0:1, :] = g_idx * EXPERTS_PER_GROUP + i1
    idx_ref[1:2, :] = g_idx * EXPERTS_PER_GROUP + i2
    wt_ref[0:1, :] = g_w * p1 / denom
    wt_ref[1:2, :] = g_w * p2 / denom


def _outproj(y, o_m, o_g, x, lw):
    n_tok = x.shape[0]
    tm = min(512, n_tok)

    def row(i):
        return (i, 0)

    def full(i):
        return (0, 0)

    consts = [lw['w_glu'], lw['b_glu'], lw['out_norm_ssm'], lw['out_norm_mla'], lw['out_norm_gqa'],
              lw['w_out_s'], lw['w_out_m'], lw['w_out_g'], lw['ffn_norm'],
              lw['w_r_hi'], lw['w_r_lo'], lw['r_bias']]
    in_specs = [pl.BlockSpec((tm, SSM_WIDTH), row), pl.BlockSpec((tm, MLA_WIDTH), row),
                pl.BlockSpec((tm, GQA_WIDTH), row), pl.BlockSpec((tm, D_MODEL), row)]
    in_specs += [pl.BlockSpec(c.shape, full) for c in consts]
    return pl.pallas_call(
        _outproj_kernel,
        grid=(n_tok // tm,),
        in_specs=in_specs,
        out_specs=[pl.BlockSpec((tm, D_MODEL), row), pl.BlockSpec((tm, D_MODEL), row),
                   pl.BlockSpec((2, tm), lambda i: (0, i)), pl.BlockSpec((2, tm), lambda i: (0, i))],
        out_shape=[jax.ShapeDtypeStruct((n_tok, D_MODEL), F32), jax.ShapeDtypeStruct((n_tok, D_MODEL), BF16),
                   jax.ShapeDtypeStruct((2, n_tok), jnp.int32), jax.ShapeDtypeStruct((2, n_tok), F32)],
        compiler_params=_cparams("parallel"),
        name="outproj_router",
    )(y, o_m, o_g, x, *consts)


def _expert_kernel(be_ref, nu_ref, x_ref, wg_ref, wu_ref, wd_ref, o_ref, wg_s, wu_s, wd_s):
    i = pl.program_id(0)

    @pl.when(jnp.logical_or(i == 0, be_ref[i] != be_ref[jnp.maximum(i - 1, 0)]))
    def _():
        wg_s[...] = wg_ref[0].astype(BF16)
        wu_s[...] = wu_ref[0].astype(BF16)
        wd_s[...] = wd_ref[0].astype(BF16)

    @pl.when(i < nu_ref[0])
    def _():
        xb = x_ref[...]
        hg = _dot(xb, wg_s[...])
        hu = _dot(xb, wu_s[...])
        act = (hg * jax.nn.sigmoid(hg) * hu).astype(BF16)
        o_ref[...] = _dot(act, wd_s[...]).astype(BF16)

    @pl.when(i >= nu_ref[0])
    def _():
        o_ref[...] = jnp.zeros_like(o_ref)


def _experts(buf, block_expert, n_used, layer, w_gate, w_up, w_down):
    rows = buf.shape[0]
    n_blocks = rows // MOE_TILE
    grid_spec = pltpu.PrefetchScalarGridSpec(
        num_scalar_prefetch=2,
        grid=(n_blocks,),
        in_specs=[pl.BlockSpec((MOE_TILE, D_MODEL), lambda i, be, nu: (i, 0)),
                  pl.BlockSpec((None, 1, D_MODEL, EXPERT_FF), lambda i, be, nu: (layer, be[i], 0, 0)),
                  pl.BlockSpec((None, 1, D_MODEL, EXPERT_FF), lambda i, be, nu: (layer, be[i], 0, 0)),
                  pl.BlockSpec((None, 1, EXPERT_FF, D_MODEL), lambda i, be, nu: (layer, be[i], 0, 0))],
        out_specs=pl.BlockSpec((MOE_TILE, D_MODEL), lambda i, be, nu: (i, 0)),
        scratch_shapes=[pltpu.VMEM((D_MODEL, EXPERT_FF), BF16), pltpu.VMEM((D_MODEL, EXPERT_FF), BF16),
                        pltpu.VMEM((EXPERT_FF, D_MODEL), BF16)],
    )
    return pl.pallas_call(
        _expert_kernel,
        grid_spec=grid_spec,
        out_shape=jax.ShapeDtypeStruct((rows, D_MODEL), BF16),
        compiler_params=_cparams("arbitrary"),
        name="experts",
    )(block_expert, n_used, buf, w_gate, w_up, w_down)


def _combine_kernel(x_ref, g0_ref, g1_ref, w_ref, gain_ref, o_ref, *, final):
    w = w_ref[...]
    y = x_ref[...] + w[:, 0:1] * g0_ref[...].astype(F32) + w[:, 1:2] * g1_ref[...].astype(F32)
    if final:
        y = _rms(y, gain_ref[...])
    o_ref[...] = y


def _combine(x, g0, g1, w_col, gain, final):
    n_tok = x.shape[0]
    tm = min(512, n_tok)

    def row(i):
        return (i, 0)

    return pl.pallas_call(
        functools.partial(_combine_kernel, final=final),
        grid=(n_tok // tm,),
        in_specs=[pl.BlockSpec((tm, D_MODEL), row), pl.BlockSpec((tm, D_MODEL), row),
                  pl.BlockSpec((tm, D_MODEL), row), pl.BlockSpec((tm, 2), row),
                  pl.BlockSpec((1, D_MODEL), lambda i: (0, 0))],
        out_specs=pl.BlockSpec((tm, D_MODEL), row),
        out_shape=jax.ShapeDtypeStruct((n_tok, D_MODEL), F32),
        compiler_params=_cparams("parallel"),
        name="moe_combine",
    )(x, g0, g1, w_col, gain)


def _moe(h, idx, wts, x, lw, gain, final):
    n_tok = h.shape[0]
    n_assign = 2 * n_tok
    i32 = jnp.int32
    flat_e = idx.reshape(-1)
    experts = jnp.arange(N_EXPERTS, dtype=i32)
    order = jnp.argsort(flat_e).astype(i32)
    counts = jnp.sum((flat_e[:, None] == experts[None, :]).astype(i32), axis=0)
    padded = (counts + MOE_TILE - 1) // MOE_TILE * MOE_TILE
    end = jnp.cumsum(counts)
    start = end - counts
    pad_end = jnp.cumsum(padded)
    pad_start = pad_end - padded
    n_blocks = -(-n_assign // MOE_TILE) + N_EXPERTS
    rows = n_blocks * MOE_TILE
    blk0 = jnp.arange(n_blocks, dtype=i32) * MOE_TILE
    block_expert = jnp.minimum(jnp.sum((pad_end[None, :] <= blk0[:, None]).astype(i32), axis=1), N_EXPERTS - 1)
    n_used = (pad_end[-1] // MOE_TILE).astype(i32).reshape(1)
    row_e = jnp.repeat(block_expert, MOE_TILE)
    local = jnp.arange(rows, dtype=i32) - pad_start[row_e]
    sorted_idx = jnp.clip(start[row_e] + local, 0, n_assign - 1)
    src_tok = jnp.where(local < counts[row_e], order[sorted_idx] % n_tok, 0)
    sorted_pos = jnp.arange(n_assign, dtype=i32)
    sorted_e = jnp.sum((end[None, :] <= sorted_pos[:, None]).astype(i32), axis=1)
    dest = pad_start[sorted_e] + sorted_pos - start[sorted_e]
    _, pos = lax.sort_key_val(order, dest)
    buf = jnp.take(h, src_tok, axis=0)
    out = _experts(buf, block_expert, n_used, lw['layer'], lw['w_gate'], lw['w_up'], lw['w_down'])
    g0 = jnp.take(out, pos[:n_tok], axis=0)
    g1 = jnp.take(out, pos[n_tok:], axis=0)
    return _combine(x, g0, g1, wts.T, gain, final)


def _rope_tables(seq_len):
    t = jnp.arange(seq_len)
    row_idx = (t // GRID_W).astype(F32)
    col_idx = (t % GRID_W).astype(F32)

    def cos_sin(rot_dim):
        n_freq = rot_dim // 4
        inv_freq = ROPE_THETA ** (-jnp.arange(n_freq, dtype=F32) / n_freq)
        ang = jnp.concatenate([row_idx[:, None] * inv_freq, col_idx[:, None] * inv_freq], axis=-1)
        return jnp.cos(ang), jnp.sin(ang)

    cm, sm = cos_sin(MLA_ROPE)
    cg, sg = cos_sin(GQA_HEAD_DIM)
    one = jnp.ones((seq_len, MLA_NOPE), F32)
    zero = jnp.zeros((seq_len, MLA_NOPE), F32)
    pad = jnp.zeros((seq_len, SLOT - MLA_NOPE - MLA_ROPE), F32)
    mla_c = jnp.concatenate([one, cm, cm, pad], axis=-1)
    mla_s = jnp.concatenate([zero, sm, sm, pad], axis=-1)
    gqa_c = jnp.concatenate([cg, cg, cg, cg], axis=-1)
    gqa_s = jnp.concatenate([-sg, sg, -sg, sg], axis=-1)
    return mla_c, mla_s, gqa_c, gqa_s


def _s5_weights(lam_re, lam_im, log_dt, b_re, b_im, c_re, c_im, d_skip):
    t_len = S5_CHUNK
    lam = lax.complex(lam_re.astype(F32), lam_im.astype(F32))
    dt = jnp.exp(log_dt.astype(F32))[..., None]
    lam_dt = lam * dt
    lam_bar = jnp.exp(lam_dt)
    b_bar = ((lam_bar - 1.0) / lam)[..., None] * lax.complex(b_re.astype(F32), b_im.astype(F32))
    c_mat = lax.complex(c_re.astype(F32), c_im.astype(F32))
    k_idx = jnp.arange(t_len + 1, dtype=F32)
    pw = jnp.exp(lam_dt[:, None] * k_idx[None, :, None, None])
    kern = jnp.einsum('dgcp,dkgp,dgpe->dkgce', c_mat, pw[:, :t_len], b_bar).real
    d_eye = jnp.eye(SSM_GROUP, dtype=F32) * d_skip.astype(F32).reshape(SSM_GROUPS, SSM_GROUP, 1)
    t_i = jnp.arange(t_len)
    table = jnp.concatenate([kern[1][:0:-1], (kern[0][0] + kern[1][0] + d_eye)[None], kern[0][1:]], axis=0)
    table = table.transpose(1, 3, 0, 2).reshape(SSM_GROUPS, SSM_GROUP, (2 * t_len - 1) * SSM_GROUP).astype(BF16)
    period = 2 * t_len * SSM_GROUP
    table = jnp.pad(table, ((0, 0), (0, 0), (0, SSM_GROUP)))
    table = jnp.roll(table, -(t_len - 1) * SSM_GROUP, axis=-1)
    flat = jnp.tile(table, (1, 1, t_len))[:, :, :t_len * (period - SSM_GROUP)]
    toep = flat.reshape(SSM_GROUPS, SSM_GROUP, t_len, period - SSM_GROUP)[..., :S5_K]
    toep = toep.transpose(0, 2, 1, 3).reshape(SSM_GROUPS, S5_K, S5_K)
    wf = pw[0][t_len - 1 - t_i][:, :, :, None] * b_bar[0][None]
    wb = pw[1][t_i][:, :, :, None] * b_bar[1][None]

    def state_cols(w):
        w = w.transpose(1, 0, 3, 2).reshape(SSM_GROUPS, S5_K, SSM_STATE)
        return jnp.concatenate([w.real, w.imag, w.imag, w.real], axis=-1)

    w_s = jnp.concatenate([state_cols(wf), state_cols(wb)], axis=-1)
    yf = c_mat[0][None] * pw[0][t_i + 1][:, :, None, :]
    yb = c_mat[1][None] * pw[1][t_len - t_i][:, :, None, :]

    def out_rows(w):
        w = w.transpose(1, 3, 0, 2).reshape(SSM_GROUPS, SSM_STATE, S5_K)
        return jnp.concatenate([w.real, -w.imag], axis=1)

    w_y = jnp.concatenate([out_rows(yf), out_rows(yb)], axis=1)
    a = pw[:, t_len]
    ar, ai = a.real, a.imag

    def coef(d):
        return jnp.stack([jnp.concatenate([ar[d], ar[d]], -1), jnp.concatenate([-ai[d], ai[d]], -1),
                          jnp.concatenate([ai[d], -ai[d]], -1)], axis=1)

    s5_coef = jnp.concatenate([coef(0), coef(1)], axis=-1)
    return toep.astype(BF16), w_s.astype(BF16), w_y.astype(BF16), s5_coef


def _static_mats():
    bd = np.kron(np.eye(GQA_HEADS), np.full((GQA_HEAD_DIM, GQA_HEAD_DIM), 1.0 / GQA_HEAD_DIM))
    half = GQA_HEAD_DIM // 2
    swap = np.zeros((GQA_HEAD_DIM, GQA_HEAD_DIM))
    swap[np.arange(half) + half, np.arange(half)] = 1.0
    swap[np.arange(half), np.arange(half) + half] = 1.0
    perm = np.kron(np.eye(GQA_HEADS), swap)
    place = np.zeros((GQA_WIDTH, GQA_HEADS * SLOT))
    for hd in range(GQA_HEADS):
        kv = hd // (GQA_HEADS // GQA_KV_HEADS)
        for d in range(GQA_HEAD_DIM):
            place[hd * GQA_HEAD_DIM + d, hd * SLOT + kv * GQA_HEAD_DIM + d] = 1.0
    return (jnp.asarray(bd, BF16), jnp.asarray(perm, BF16), jnp.asarray(place, BF16))


def _layer_weights(l, p):
    f = lambda name: p[name][l].astype(F32)
    lw = {}
    w_in = f('w_in')
    u, c_q, c_kv, k_pe, q_g, k_g, v_g = jnp.split(w_in, [256, 512, 640, 672, 1056, 1184], axis=-1)
    z = lambda n: jnp.zeros((D_MODEL, n), F32)
    half = MLA_ROPE // 2
    kpe_slot = jnp.concatenate([z(MLA_NOPE), k_pe, z(32)], axis=-1)
    kpe_rot = jnp.concatenate([z(MLA_NOPE), -k_pe[:, half:], k_pe[:, :half], z(32)], axis=-1)
    lw['w_in'] = jnp.concatenate([u, c_q, c_kv, kpe_slot, kpe_rot, q_g, k_g, v_g], axis=-1).astype(BF16)
    lw['attn_norm'] = f('attn_norm')[None]
    lw['mla_q_norm'] = f('mla_q_norm')[None]
    lw['mla_kv_norm'] = f('mla_kv_norm')[None]
    w_uq = f('mla_w_uq').reshape(MLA_Q_LORA, MLA_HEADS, MLA_NOPE + MLA_ROPE)
    nope, pe = w_uq[..., :MLA_NOPE], w_uq[..., MLA_NOPE:]
    zq = jnp.zeros((MLA_Q_LORA, MLA_HEADS, 32), F32)
    lw['w_uq'] = jnp.concatenate([nope, pe, zq], axis=-1).reshape(MLA_Q_LORA, -1).astype(BF16)
    lw['w_uq_rot'] = jnp.concatenate([jnp.zeros_like(nope), -pe[..., half:], pe[..., :half], zq],
                                     axis=-1).reshape(MLA_Q_LORA, -1).astype(BF16)
    w_ukv = f('mla_w_ukv').reshape(MLA_KV_LORA, MLA_HEADS, MLA_NOPE + MLA_V)
    zk = jnp.zeros((MLA_KV_LORA, MLA_HEADS, SLOT - MLA_NOPE), F32)
    lw['w_ukv_k'] = jnp.concatenate([w_ukv[..., :MLA_NOPE], zk], axis=-1).reshape(MLA_KV_LORA, -1).astype(BF16)
    lw['w_ukv_v'] = w_ukv[..., MLA_NOPE:].reshape(MLA_KV_LORA, -1).astype(BF16)
    lw['gqa_q_norm'] = jnp.tile(f('gqa_q_norm'), GQA_HEADS)[None]
    lw['gqa_k_norm'] = jnp.tile(f('gqa_k_norm'), GQA_KV_HEADS)[None]
    lw['bd'], lw['perm'], lw['place'] = _static_mats()
    lw['s5_toep'], lw['s5_w_s'], lw['s5_w_y'], lw['s5_coef'] = _s5_weights(
        p['s5_lam_re'][l], p['s5_lam_im'][l], p['s5_log_dt'][l], p['s5_b_re'][l], p['s5_b_im'][l],
        p['s5_c_re'][l], p['s5_c_im'][l], p['s5_d'][l])
    lw['w_glu'] = f('s5_w_glu').astype(BF16)
    lw['b_glu'] = f('s5_b_glu')[None]
    lw['out_norm_ssm'] = f('out_norm_ssm')[None]
    lw['out_norm_mla'] = f('out_norm_mla')[None]
    lw['out_norm_gqa'] = f('out_norm_gqa')[None]
    w_out = f('w_out')
    lw['w_out_s'] = w_out[:SSM_WIDTH].astype(BF16)
    lw['w_out_m'] = w_out[SSM_WIDTH:SSM_WIDTH + MLA_WIDTH].astype(BF16)
    lw['w_out_g'] = w_out[SSM_WIDTH + MLA_WIDTH:].astype(BF16)
    lw['ffn_norm'] = f('ffn_norm')[None]
    w_r = jnp.concatenate([f('router_group_w'), jnp.zeros((D_MODEL, 4), F32), f('router_expert_w'),
                           jnp.zeros((D_MODEL, LANE - 8 - N_EXPERTS), F32)], axis=-1)
    lw['w_r_hi'], lw['w_r_lo'] = _split_bf16(w_r)
    lw['r_bias'] = jnp.concatenate([f('router_group_b'), jnp.full((4,), -1e30, F32), f('router_expert_b'),
                                    jnp.zeros((LANE - 8 - N_EXPERTS,), F32)])[None]
    lw['layer'] = l
    lw['w_gate'] = p['expert_w_gate'].astype(F32)
    lw['w_up'] = p['expert_w_up'].astype(F32)
    lw['w_down'] = p['expert_w_down'].astype(F32)
    return lw


def _trunk(x, layers, final_gain):
    bsz, seq_len, _ = x.shape
    tabs = _rope_tables(seq_len)
    xf = x.reshape(bsz * seq_len, D_MODEL).astype(F32)
    depth = len(layers)
    for l, lw in enumerate(layers):
        u, q_m, k_m, v_m, q_g, k_g, v_g = _inproj(xf, lw, tabs, seq_len)
        y = _s5(u, lw, bsz, seq_len)
        o_m = _attention(q_m, k_m, v_m, bsz, seq_len, groups=MLA_HEADS // 2, qslots=((0,), (1,)),
                         kslots=2, kslot=(0, 1), tq=512)
        o_g = _attention(q_g, k_g, v_g, bsz, seq_len, groups=1, qslots=((0, 1, 2), (3, 4, 5)),
                         kslots=1, kslot=(0, 0), tq=256)
        x_mid, h, idx, wts = _outproj(y, o_m, o_g, xf, lw)
        xf = _moe(h, idx, wts, x_mid, lw, final_gain, final=(l == depth - 1))
    return xf.reshape(bsz, seq_len, D_MODEL)


def kernel(x_prompt, x_sample, attn_norm, w_in, s5_lam_re, s5_lam_im, s5_log_dt, s5_b_re, s5_b_im, s5_c_re, s5_c_im, s5_d, s5_w_glu, s5_b_glu, mla_q_norm, mla_w_uq, mla_kv_norm, mla_w_ukv, gqa_q_norm, gqa_k_norm, out_norm_ssm, out_norm_mla, out_norm_gqa, w_out, ffn_norm, router_group_w, router_group_b, router_expert_w, router_expert_b, expert_w_gate, expert_w_up, expert_w_down, final_norm):
    p = dict(attn_norm=attn_norm, w_in=w_in, s5_lam_re=s5_lam_re, s5_lam_im=s5_lam_im, s5_log_dt=s5_log_dt,
             s5_b_re=s5_b_re, s5_b_im=s5_b_im, s5_c_re=s5_c_re, s5_c_im=s5_c_im, s5_d=s5_d,
             s5_w_glu=s5_w_glu, s5_b_glu=s5_b_glu, mla_q_norm=mla_q_norm, mla_w_uq=mla_w_uq,
             mla_kv_norm=mla_kv_norm, mla_w_ukv=mla_w_ukv, gqa_q_norm=gqa_q_norm, gqa_k_norm=gqa_k_norm,
             out_norm_ssm=out_norm_ssm, out_norm_mla=out_norm_mla, out_norm_gqa=out_norm_gqa, w_out=w_out,
             ffn_norm=ffn_norm, router_group_w=router_group_w, router_group_b=router_group_b,
             router_expert_w=router_expert_w, router_expert_b=router_expert_b,
             expert_w_gate=expert_w_gate, expert_w_up=expert_w_up, expert_w_down=expert_w_down)
    depth = w_in.shape[0]
    layers = [_layer_weights(l, p) for l in range(depth)]
    final_gain = final_norm.astype(F32)[None]
    return (_trunk(x_prompt, layers, final_gain), _trunk(x_sample, layers, final_gain))
```

```python
import functools
import math

import jax
import jax.numpy as jnp
import numpy as np
from jax import lax
from jax.experimental import pallas as pl
from jax.experimental.pallas import tpu as pltpu

F32 = jnp.float32
BF16 = jnp.bfloat16

D_MODEL = 1024
GRID_W = 64
NORM_EPS = 1e-6
ROPE_THETA = 10000.0

SSM_WIDTH = 256
SSM_GROUP = 16
SSM_GROUPS = 16
SSM_STATE = 64

MLA_HEADS = 6
MLA_NOPE = 64
MLA_ROPE = 32
MLA_V = 64
MLA_Q_LORA = 256
MLA_KV_LORA = 128
MLA_WIDTH = MLA_HEADS * MLA_V

GQA_HEADS = 6
GQA_KV_HEADS = 2
GQA_HEAD_DIM = 64
GQA_WIDTH = GQA_HEADS * GQA_HEAD_DIM

N_EXPERT_GROUPS = 4
EXPERTS_PER_GROUP = 8
N_EXPERTS = 32
EXPERT_FF = 512

LANE = 128
SUBLANES = 8
SLOT = 128
S5_CHUNK = 64
S5_K = S5_CHUNK * SSM_GROUP
S5_SW = 512
S5_HW = 256
S5_ROWS = 512
S5_CARRY_TB = 4
WIN_PAD = 1536
MOE_TILE = 256
VMEM_LIMIT = 48 * 1024 * 1024
LOG2E = math.log2(math.e)

_C_U, _C_CQ, _C_CKV, _C_KPE, _C_KPER, _C_QG, _C_KG, _C_VG = 0, 256, 512, 640, 768, 896, 1280, 1408


def _cparams(*sem):
    return pltpu.CompilerParams(dimension_semantics=sem, vmem_limit_bytes=VMEM_LIMIT)


def _split_bf16(x):
    hi = x.astype(BF16)
    lo = (x - hi.astype(F32)).astype(BF16)
    return hi, lo


def _dot(a, b):
    return jnp.dot(a, b, preferred_element_type=F32)


def _dot2(x, w):
    hi, lo = _split_bf16(x)
    return _dot(hi, w) + _dot(lo, w)


def _rms(x, gain):
    return x * lax.rsqrt(jnp.mean(x * x, axis=-1, keepdims=True) + NORM_EPS) * gain


def _inproj_kernel(x_ref, g_ref, win_ref, qng_ref, wuq_ref, wuqr_ref, kvng_ref, wukk_ref, wukv_ref,
                   gqn_ref, gkn_ref, bd_ref, perm_ref, place_ref,
                   mc_ref, ms_ref, gc_ref, gs_ref,
                   u_ref, qm_ref, km_ref, vm_ref, qg_ref, kg_ref, vg_ref):
    h = _rms(x_ref[...], g_ref[...]).astype(BF16)
    proj = _dot(h, win_ref[...])
    for g in range(SSM_GROUPS):
        u_ref[g] = proj[:, _C_U + g * SSM_GROUP:_C_U + (g + 1) * SSM_GROUP].astype(BF16)
    vg_ref[...] = proj[:, _C_VG:_C_VG + 128].astype(BF16)

    mc = mc_ref[...]
    ms = ms_ref[...]
    cq = _rms(proj[:, _C_CQ:_C_CQ + MLA_Q_LORA], qng_ref[...]).astype(BF16)
    qa = _dot(cq, wuq_ref[...])
    qb = _dot(cq, wuqr_ref[...])
    scale = (MLA_NOPE + MLA_ROPE) ** -0.5 * LOG2E
    for hd in range(MLA_HEADS):
        sl = slice(hd * SLOT, (hd + 1) * SLOT)
        qm_ref[:, sl] = ((qa[:, sl] * mc + qb[:, sl] * ms) * scale).astype(BF16)
    ckv = _rms(proj[:, _C_CKV:_C_CKV + MLA_KV_LORA], kvng_ref[...]).astype(BF16)
    kk = _dot(ckv, wukk_ref[...])
    kpe = proj[:, _C_KPE:_C_KPE + SLOT] * mc + proj[:, _C_KPER:_C_KPER + SLOT] * ms
    for hd in range(MLA_HEADS):
        sl = slice(hd * SLOT, (hd + 1) * SLOT)
        km_ref[:, sl] = (kk[:, sl] + kpe).astype(BF16)
    vm_ref[...] = _dot(ckv, wukv_ref[...]).astype(BF16)

    gc = gc_ref[...]
    gs = gs_ref[...]
    bd = bd_ref[...]
    perm = perm_ref[...]
    qg = proj[:, _C_QG:_C_QG + GQA_WIDTH]
    qn = qg * lax.rsqrt(_dot2(qg * qg, bd) + NORM_EPS) * gqn_ref[...]
    gc3 = jnp.concatenate([gc, gc, gc], axis=-1)
    gs3 = jnp.concatenate([gs, gs, gs], axis=-1)
    qr = (qn * gc3 + _dot2(qn, perm) * gs3) * (GQA_HEAD_DIM ** -0.5 * LOG2E)
    qg_ref[...] = _dot(qr.astype(BF16), place_ref[...]).astype(BF16)
    kg = proj[:, _C_KG:_C_KG + 128]
    kn = kg * lax.rsqrt(_dot2(kg * kg, bd[:128, :128]) + NORM_EPS) * gkn_ref[...]
    kg_ref[...] = (kn * gc + _dot2(kn, perm[:128, :128]) * gs).astype(BF16)


def _inproj(x, lw, tabs, seq_len):
    n_tok = x.shape[0]
    tm = min(512, seq_len)
    nt = n_tok // tm
    per_seq = seq_len // tm

    def row(i):
        return (i, 0)

    def full(i):
        return (0, 0)

    def tab(i):
        return (i % per_seq, 0)

    consts = [lw['attn_norm'], lw['w_in'], lw['mla_q_norm'], lw['w_uq'], lw['w_uq_rot'],
              lw['mla_kv_norm'], lw['w_ukv_k'], lw['w_ukv_v'], lw['gqa_q_norm'], lw['gqa_k_norm'],
              lw['bd'], lw['perm'], lw['place']]
    in_specs = [pl.BlockSpec((tm, D_MODEL), row)]
    in_specs += [pl.BlockSpec(c.shape, full) for c in consts]
    in_specs += [pl.BlockSpec((tm, LANE), tab)] * 4
    widths = [MLA_HEADS * SLOT, MLA_HEADS * SLOT, MLA_WIDTH, GQA_HEADS * SLOT, 128, 128]
    u_spec = pl.BlockSpec((SSM_GROUPS, tm, SSM_GROUP), lambda i: (0, i, 0))
    return pl.pallas_call(
        _inproj_kernel,
        grid=(nt,),
        in_specs=in_specs,
        out_specs=[u_spec] + [pl.BlockSpec((tm, w), row) for w in widths],
        out_shape=[jax.ShapeDtypeStruct((SSM_GROUPS, n_tok, SSM_GROUP), BF16)]
        + [jax.ShapeDtypeStruct((n_tok, w), BF16) for w in widths],
        compiler_params=_cparams("parallel"),
        name="inproj",
    )(x, *consts, *tabs)


def _s5_state_kernel(u_ref, w_ref, s_ref, *, n_chunks, tb):
    res = _dot(u_ref[0], w_ref[0])
    for bb in range(tb):
        s_ref[:, bb * S5_SW:(bb + 1) * S5_SW] = res[bb * n_chunks:(bb + 1) * n_chunks]


def _s5_state(u_t, w_s, n_chunks, tb):
    n_grp, rows, _ = u_t.shape
    n_bt = rows // (tb * n_chunks)
    return pl.pallas_call(
        functools.partial(_s5_state_kernel, n_chunks=n_chunks, tb=tb),
        grid=(n_grp, n_bt),
        in_specs=[pl.BlockSpec((1, tb * n_chunks, S5_K), lambda g, r: (g, r, 0)),
                  pl.BlockSpec((1, S5_K, S5_SW), lambda g, r: (g, 0, 0))],
        out_specs=pl.BlockSpec((n_chunks, tb * S5_SW), lambda g, r: (0, g * n_bt + r)),
        out_shape=jax.ShapeDtypeStruct((n_chunks, n_grp * n_bt * tb * S5_SW), F32),
        compiler_params=_cparams("parallel", "parallel"),
        name="s5_state",
    )(u_t, w_s)


def _s5_carry_kernel(s_ref, c_ref, h_ref, *, n_chunks, tb):
    a1f, a2f, a3f = c_ref[0, 0:1, 0:LANE], c_ref[0, 1:2, 0:LANE], c_ref[0, 2:3, 0:LANE]
    a1b, a2b, a3b = c_ref[0, 0:1, LANE:], c_ref[0, 1:2, LANE:], c_ref[0, 2:3, LANE:]
    zero = jnp.zeros((1, LANE), F32)
    th = SUBLANES if n_chunks % SUBLANES == 0 else n_chunks
    n_tiles = n_chunks // th
    row = lax.broadcasted_iota(jnp.int32, (th, LANE), 0)

    def body(it, carry):
        base_f = pl.multiple_of(it * th, th)
        base_b = pl.multiple_of((n_tiles - 1 - it) * th, th)
        new = []
        for bb in range(tb):
            hf, hfs, hb, hbs = carry[4 * bb:4 * bb + 4]
            s0 = bb * S5_SW
            h0 = bb * S5_HW
            sf = s_ref[pl.ds(base_f, th), s0:s0 + LANE]
            sfs = s_ref[pl.ds(base_f, th), s0 + LANE:s0 + 2 * LANE]
            sb = s_ref[pl.ds(base_b, th), s0 + 2 * LANE:s0 + 3 * LANE]
            sbs = s_ref[pl.ds(base_b, th), s0 + 3 * LANE:s0 + 4 * LANE]
            out_f = jnp.zeros((th, LANE), F32)
            out_b = jnp.zeros((th, LANE), F32)
            for r in range(th):
                rb = th - 1 - r
                out_f = jnp.where(row == r, hf, out_f)
                out_b = jnp.where(row == rb, hb, out_b)
                hf, hfs = (a1f * hf + a2f * hfs + sf[r:r + 1], a1f * hfs + a3f * hf + sfs[r:r + 1])
                hb, hbs = (a1b * hb + a2b * hbs + sb[rb:rb + 1], a1b * hbs + a3b * hb + sbs[rb:rb + 1])
            h_ref[pl.ds(base_f, th), h0:h0 + LANE] = out_f
            h_ref[pl.ds(base_b, th), h0 + LANE:h0 + 2 * LANE] = out_b
            new += [hf, hfs, hb, hbs]
        return tuple(new)

    lax.fori_loop(0, n_tiles, body, (zero,) * (4 * tb))


def _s5_carry(s, coef, n_chunks, tb):
    n_grp = coef.shape[0]
    n_bt = s.shape[1] // (n_grp * tb * S5_SW)
    return pl.pallas_call(
        functools.partial(_s5_carry_kernel, n_chunks=n_chunks, tb=tb),
        grid=(n_grp, n_bt),
        in_specs=[pl.BlockSpec((n_chunks, tb * S5_SW), lambda g, r: (0, g * n_bt + r)),
                  pl.BlockSpec((1, 3, S5_HW), lambda g, r: (g, 0, 0))],
        out_specs=pl.BlockSpec((n_chunks, tb * S5_HW), lambda g, r: (0, g * n_bt + r)),
        out_shape=jax.ShapeDtypeStruct((n_chunks, n_grp * n_bt * tb * S5_HW), F32),
        compiler_params=_cparams("parallel", "parallel"),
        name="s5_carry",
    )(s, coef)


def _gelu_tanh(y):
    return 0.5 * y * (1.0 + jnp.tanh(math.sqrt(2.0 / math.pi) * (y + 0.044715 * (y * y * y))))


def _s5_out_kernel(u_ref, h_ref, toep_ref, wy_ref, y_ref, *, tb):
    h = jnp.concatenate([h_ref[:, bb * S5_HW:(bb + 1) * S5_HW] for bb in range(tb)], axis=0)
    y = _dot(u_ref[0], toep_ref[0]) + _dot(h.astype(BF16), wy_ref[0])
    y_ref[0] = _gelu_tanh(y).astype(BF16)


def _s5_out(u_t, h_in, toep, w_y, n_chunks, tb):
    n_grp, rows, _ = u_t.shape
    n_bt = rows // (tb * n_chunks)
    return pl.pallas_call(
        functools.partial(_s5_out_kernel, tb=tb),
        grid=(n_grp, n_bt),
        in_specs=[pl.BlockSpec((1, tb * n_chunks, S5_K), lambda g, r: (g, r, 0)),
                  pl.BlockSpec((n_chunks, tb * S5_HW), lambda g, r: (0, g * n_bt + r)),
                  pl.BlockSpec((1, S5_K, S5_K), lambda g, r: (g, 0, 0)),
                  pl.BlockSpec((1, S5_HW, S5_K), lambda g, r: (g, 0, 0))],
        out_specs=pl.BlockSpec((1, tb * n_chunks, S5_K), lambda g, r: (g, r, 0)),
        out_shape=jax.ShapeDtypeStruct((n_grp, rows, S5_K), BF16),
        compiler_params=_cparams("parallel", "parallel"),
        name="s5_out",
    )(u_t, h_in, toep, w_y)


def _s5(u, lw, bsz, seq_len):
    n_chunks = seq_len // S5_CHUNK
    tb = max(1, min(bsz, S5_ROWS // n_chunks))
    assert bsz % tb == 0
    u_t = u.reshape(SSM_GROUPS, bsz * n_chunks, S5_K)
    s = _s5_state(u_t, lw['s5_w_s'], n_chunks, tb)
    h_in = _s5_carry(s, lw['s5_coef'], n_chunks, math.gcd(tb, S5_CARRY_TB))
    y_t = _s5_out(u_t, h_in, lw['s5_toep'], lw['s5_w_y'], n_chunks, tb)
    return y_t.reshape(SSM_GROUPS, bsz * seq_len, SSM_GROUP)


def _attn_kernel(q_ref, k_ref, vt_ref, o_ref, m_ref, a_ref, acc_ref, s_ref, *, qslots, kslot, tk, n_kv):
    tq = q_ref.shape[0]
    n_virt = len(qslots)
    m_ref[...] = jnp.full(m_ref.shape, -jnp.inf, F32)
    acc_ref[...] = jnp.zeros(acc_ref.shape, F32)

    def scores(j, i):
        q = jnp.concatenate([q_ref[:, s * SLOT:(s + 1) * SLOT] for s in qslots[j]], axis=0)
        start = pl.multiple_of(i * tk, tk)
        kb = k_ref[pl.ds(start, tk), kslot[j] * SLOT:(kslot[j] + 1) * SLOT]
        st = lax.dot_general(kb, q, (((1,), (1,)), ((), ())), preferred_element_type=F32)
        s_ref[j] = st
        m_old = m_ref[j]
        m_new = jnp.maximum(m_old, jnp.max(st, axis=0, keepdims=True))
        m_ref[j] = m_new
        a_ref[j] = jnp.exp2(m_old - m_new)

    def values(j, i):
        pt = jnp.exp2(s_ref[j] - m_ref[j]).astype(BF16)
        acc_ref[j] = a_ref[j] * acc_ref[j] + _dot(vt_ref[j, i], pt)

    scores(0, 0)

    def body(i, carry):
        scores(1, i)
        values(0, i)
        scores(0, i + 1)
        values(1, i)
        return carry

    lax.fori_loop(0, n_kv - 1, body, 0, unroll=True if n_kv <= 5 else 4)
    scores(1, n_kv - 1)
    values(0, n_kv - 1)
    values(1, n_kv - 1)
    pieces = []
    for j in range(n_virt):
        acc = acc_ref[j]
        o = acc[0:64] / acc[64:65]
        for n in range(len(qslots[j])):
            pieces.append(o[:, n * tq:(n + 1) * tq])
    for a in range(len(pieces) // 2):
        pair = jnp.concatenate([pieces[2 * a], pieces[2 * a + 1]], axis=0)
        o_ref[:, a * SLOT:(a + 1) * SLOT] = pair.T


V_ROWS = 80


def _attention(q, k, v, bsz, seq_len, *, groups, qslots, kslots, kslot, tq):
    tq = min(tq, seq_len)
    tk = min(512, seq_len)
    nq = seq_len // tq
    n_kv = seq_len // tk
    heads = sum(len(s) for s in qslots)
    n_stack = len(qslots[0])
    q3 = q.reshape(bsz, seq_len, -1)
    k3 = k.reshape(bsz, seq_len, -1)
    vt = v.reshape(bsz, n_kv, tk, groups * 2, 64).transpose(0, 3, 1, 4, 2)
    vt = jnp.concatenate([vt, jnp.ones((bsz, groups * 2, n_kv, V_ROWS - 64, tk), BF16)], axis=3)
    ow = heads * 64
    out = pl.pallas_call(
        functools.partial(_attn_kernel, qslots=qslots, kslot=kslot, tk=tk, n_kv=n_kv),
        grid=(bsz, groups, nq),
        in_specs=[pl.BlockSpec((None, tq, heads * SLOT), lambda b, g, i: (b, i, g)),
                  pl.BlockSpec((None, seq_len, kslots * SLOT), lambda b, g, i: (b, 0, g)),
                  pl.BlockSpec((None, 2, n_kv, V_ROWS, tk), lambda b, g, i: (b, g, 0, 0, 0))],
        out_specs=pl.BlockSpec((None, tq, ow), lambda b, g, i: (b, i, g)),
        out_shape=jax.ShapeDtypeStruct((bsz, seq_len, groups * ow), F32),
        scratch_shapes=[pltpu.VMEM((2, 1, n_stack * tq), F32), pltpu.VMEM((2, 1, n_stack * tq), F32),
                        pltpu.VMEM((2, V_ROWS, n_stack * tq), F32), pltpu.VMEM((2, tk, n_stack * tq), F32)],
        compiler_params=_cparams("parallel", "parallel", "arbitrary"),
        name="attention",
    )(q3, k3, vt)
    return out.reshape(bsz * seq_len, groups * ow)


def _outproj_kernel(y_ref, om_ref, og_ref, x_ref, wglu_ref, bglu_ref, ns_ref, nm_ref, ng_ref,
                    wos_ref, wom_ref, wog_ref, fn_ref, wrh_ref, wrl_ref, rb_ref,
                    xo_ref, h_ref, idx_ref, wt_ref):
    y_bf = jnp.concatenate([y_ref[g] for g in range(SSM_GROUPS)], axis=-1)
    gate = jax.nn.sigmoid(_dot(y_bf, wglu_ref[...]) + bglu_ref[...])
    o_s = _rms(y_bf.astype(F32) * gate, ns_ref[...]).astype(BF16)
    o_m = _rms(om_ref[...], nm_ref[...]).astype(BF16)
    o_g = _rms(og_ref[...], ng_ref[...]).astype(BF16)
    x_new = x_ref[...] + _dot(o_s, wos_ref[...]) + _dot(o_m, wom_ref[...]) + _dot(o_g, wog_ref[...])
    xo_ref[...] = x_new
    h = _rms(x_new, fn_ref[...])
    h_hi, h_lo = _split_bf16(h)
    h_ref[...] = h_hi
    logits = _dot(h_hi, wrh_ref[...]) + _dot(h_lo, wrh_ref[...]) + _dot(h_hi, wrl_ref[...]) + rb_ref[...]
    lt = logits.T
    row = lax.broadcasted_iota(jnp.int32, (8, lt.shape[1]), 0)

    def first_argmax(v, vmax):
        return jnp.min(jnp.where(v == vmax, row, 8), axis=0, keepdims=True)

    gl = lt[0:8]
    gmax = jnp.max(gl, axis=0, keepdims=True)
    g_w = 1.0 / jnp.sum(jnp.exp(gl - gmax), axis=0, keepdims=True)
    g_idx = first_argmax(gl, gmax)
    el = lt[8:16]
    for g in range(1, N_EXPERT_GROUPS):
        el = jnp.where(g_idx == g, lt[8 + 8 * g:16 + 8 * g], el)
    ee = jnp.exp(el - jnp.max(el, axis=0, keepdims=True))
    ep = ee / jnp.sum(ee, axis=0, keepdims=True)
    p1 = jnp.max(ep, axis=0, keepdims=True)
    i1 = first_argmax(ep, p1)
    ep2 = jnp.where(row == i1, -1.0, ep)
    p2 = jnp.max(ep2, axis=0, keepdims=True)
    i2 = first_argmax(ep2, p2)
    denom = p1 + p2
    idx_ref[0:1, :] = g_idx * EXPERTS_PER_GROUP + i1
    idx_ref[1:2, :] = g_idx * EXPERTS_PER_GROUP + i2
    wt_ref[0:1, :] = g_w * p1 / denom
    wt_ref[1:2, :] = g_w * p2 / denom


def _outproj(y, o_m, o_g, x, lw):
    n_tok = x.shape[0]
    tm = min(512, n_tok)

    def row(i):
        return (i, 0)

    def full(i):
        return (0, 0)

    consts = [lw['w_glu'], lw['b_glu'], lw['out_norm_ssm'], lw['out_norm_mla'], lw['out_norm_gqa'],
              lw['w_out_s'], lw['w_out_m'], lw['w_out_g'], lw['ffn_norm'],
              lw['w_r_hi'], lw['w_r_lo'], lw['r_bias']]
    in_specs = [pl.BlockSpec((SSM_GROUPS, tm, SSM_GROUP), lambda i: (0, i, 0)), pl.BlockSpec((tm, MLA_WIDTH), row),
                pl.BlockSpec((tm, GQA_WIDTH), row), pl.BlockSpec((tm, D_MODEL), row)]
    in_specs += [pl.BlockSpec(c.shape, full) for c in consts]
    return pl.pallas_call(
        _outproj_kernel,
        grid=(n_tok // tm,),
        in_specs=in_specs,
        out_specs=[pl.BlockSpec((tm, D_MODEL), row), pl.BlockSpec((tm, D_MODEL), row),
                   pl.BlockSpec((2, tm), lambda i: (0, i)), pl.BlockSpec((2, tm), lambda i: (0, i))],
        out_shape=[jax.ShapeDtypeStruct((n_tok, D_MODEL), F32), jax.ShapeDtypeStruct((n_tok, D_MODEL), BF16),
                   jax.ShapeDtypeStruct((2, n_tok), jnp.int32), jax.ShapeDtypeStruct((2, n_tok), F32)],
        compiler_params=_cparams("parallel"),
        name="outproj_router",
    )(y, o_m, o_g, x, *consts)


def _expert_kernel(be_ref, nu_ref, x_ref, wg_ref, wu_ref, wd_ref, o_ref, wg_s, wu_s, wd_s):
    i = pl.program_id(0)

    @pl.when(jnp.logical_or(i == 0, be_ref[i] != be_ref[jnp.maximum(i - 1, 0)]))
    def _():
        wg_s[...] = wg_ref[0].astype(BF16)
        wu_s[...] = wu_ref[0].astype(BF16)
        wd_s[...] = wd_ref[0].astype(BF16)

    @pl.when(i < nu_ref[0])
    def _():
        xb = x_ref[...]
        hg = _dot(xb, wg_s[...])
        hu = _dot(xb, wu_s[...])
        act = (hg * jax.nn.sigmoid(hg) * hu).astype(BF16)
        o_ref[...] = _dot(act, wd_s[...]).astype(BF16)

    @pl.when(i >= nu_ref[0])
    def _():
        o_ref[...] = jnp.zeros_like(o_ref)


def _experts(buf, block_expert, n_used, layer, w_gate, w_up, w_down):
    rows = buf.shape[0]
    n_blocks = rows // MOE_TILE
    grid_spec = pltpu.PrefetchScalarGridSpec(
        num_scalar_prefetch=2,
        grid=(n_blocks,),
        in_specs=[pl.BlockSpec((MOE_TILE, D_MODEL), lambda i, be, nu: (i, 0)),
                  pl.BlockSpec((None, 1, D_MODEL, EXPERT_FF), lambda i, be, nu: (layer, be[i], 0, 0)),
                  pl.BlockSpec((None, 1, D_MODEL, EXPERT_FF), lambda i, be, nu: (layer, be[i], 0, 0)),
                  pl.BlockSpec((None, 1, EXPERT_FF, D_MODEL), lambda i, be, nu: (layer, be[i], 0, 0))],
        out_specs=pl.BlockSpec((MOE_TILE, D_MODEL), lambda i, be, nu: (i, 0)),
        scratch_shapes=[pltpu.VMEM((D_MODEL, EXPERT_FF), BF16), pltpu.VMEM((D_MODEL, EXPERT_FF), BF16),
                        pltpu.VMEM((EXPERT_FF, D_MODEL), BF16)],
    )
    return pl.pallas_call(
        _expert_kernel,
        grid_spec=grid_spec,
        out_shape=jax.ShapeDtypeStruct((rows, D_MODEL), BF16),
        compiler_params=_cparams("arbitrary"),
        name="experts",
    )(block_expert, n_used, buf, w_gate, w_up, w_down)


def _combine_kernel(x_ref, g0_ref, g1_ref, w_ref, gain_ref, o_ref, *, final):
    w = w_ref[...]
    y = x_ref[...] + w[:, 0:1] * g0_ref[...].astype(F32) + w[:, 1:2] * g1_ref[...].astype(F32)
    if final:
        y = _rms(y, gain_ref[...])
    o_ref[...] = y


def _combine(x, g0, g1, w_col, gain, final):
    n_tok = x.shape[0]
    tm = min(512, n_tok)

    def row(i):
        return (i, 0)

    return pl.pallas_call(
        functools.partial(_combine_kernel, final=final),
        grid=(n_tok // tm,),
        in_specs=[pl.BlockSpec((tm, D_MODEL), row), pl.BlockSpec((tm, D_MODEL), row),
                  pl.BlockSpec((tm, D_MODEL), row), pl.BlockSpec((tm, 2), row),
                  pl.BlockSpec((1, D_MODEL), lambda i: (0, 0))],
        out_specs=pl.BlockSpec((tm, D_MODEL), row),
        out_shape=jax.ShapeDtypeStruct((n_tok, D_MODEL), F32),
        compiler_params=_cparams("parallel"),
        name="moe_combine",
    )(x, g0, g1, w_col, gain)


def _moe(h, idx, wts, x, lw, gain, final):
    n_tok = h.shape[0]
    n_assign = 2 * n_tok
    i32 = jnp.int32
    flat_e = idx.reshape(-1)
    experts = jnp.arange(N_EXPERTS, dtype=i32)
    order = jnp.argsort(flat_e).astype(i32)
    counts = jnp.sum((flat_e[:, None] == experts[None, :]).astype(i32), axis=0)
    padded = (counts + MOE_TILE - 1) // MOE_TILE * MOE_TILE
    end = jnp.cumsum(counts)
    start = end - counts
    pad_end = jnp.cumsum(padded)
    pad_start = pad_end - padded
    n_blocks = -(-n_assign // MOE_TILE) + N_EXPERTS
    rows = n_blocks * MOE_TILE
    blk0 = jnp.arange(n_blocks, dtype=i32) * MOE_TILE
    block_expert = jnp.minimum(jnp.sum((pad_end[None, :] <= blk0[:, None]).astype(i32), axis=1), N_EXPERTS - 1)
    n_used = (pad_end[-1] // MOE_TILE).astype(i32).reshape(1)
    row_e = jnp.repeat(block_expert, MOE_TILE)
    local = jnp.arange(rows, dtype=i32) - pad_start[row_e]
    sorted_idx = jnp.clip(start[row_e] + local, 0, n_assign - 1)
    src_tok = jnp.where(local < counts[row_e], order[sorted_idx] % n_tok, 0)
    sorted_pos = jnp.arange(n_assign, dtype=i32)
    sorted_e = jnp.sum((end[None, :] <= sorted_pos[:, None]).astype(i32), axis=1)
    dest = pad_start[sorted_e] + sorted_pos - start[sorted_e]
    _, pos = lax.sort_key_val(order, dest)
    buf = h.at[src_tok].get(mode='promise_in_bounds')
    out = _experts(buf, block_expert, n_used, lw['layer'], lw['w_gate'], lw['w_up'], lw['w_down'])
    g0 = out.at[pos[:n_tok]].get(mode='promise_in_bounds')
    g1 = out.at[pos[n_tok:]].get(mode='promise_in_bounds')
    return _combine(x, g0, g1, wts.T, gain, final)


def _rope_tables(seq_len):
    t = jnp.arange(seq_len)
    row_idx = (t // GRID_W).astype(F32)
    col_idx = (t % GRID_W).astype(F32)

    def cos_sin(rot_dim):
        n_freq = rot_dim // 4
        inv_freq = ROPE_THETA ** (-jnp.arange(n_freq, dtype=F32) / n_freq)
        ang = jnp.concatenate([row_idx[:, None] * inv_freq, col_idx[:, None] * inv_freq], axis=-1)
        return jnp.cos(ang), jnp.sin(ang)

    cm, sm = cos_sin(MLA_ROPE)
    cg, sg = cos_sin(GQA_HEAD_DIM)
    one = jnp.ones((seq_len, MLA_NOPE), F32)
    zero = jnp.zeros((seq_len, MLA_NOPE), F32)
    pad = jnp.zeros((seq_len, SLOT - MLA_NOPE - MLA_ROPE), F32)
    mla_c = jnp.concatenate([one, cm, cm, pad], axis=-1)
    mla_s = jnp.concatenate([zero, sm, sm, pad], axis=-1)
    gqa_c = jnp.concatenate([cg, cg, cg, cg], axis=-1)
    gqa_s = jnp.concatenate([-sg, sg, -sg, sg], axis=-1)
    return mla_c, mla_s, gqa_c, gqa_s


def _s5_weights(lam_re, lam_im, log_dt, b_re, b_im, c_re, c_im, d_skip):
    t_len = S5_CHUNK
    lam = lax.complex(lam_re.astype(F32), lam_im.astype(F32))
    dt = jnp.exp(log_dt.astype(F32))[..., None]
    lam_dt = lam * dt
    lam_bar = jnp.exp(lam_dt)
    b_bar = ((lam_bar - 1.0) / lam)[..., None] * lax.complex(b_re.astype(F32), b_im.astype(F32))
    c_mat = lax.complex(c_re.astype(F32), c_im.astype(F32))
    k_idx = jnp.arange(t_len + 1, dtype=F32)
    pw = jnp.exp(lam_dt[:, None] * k_idx[None, :, None, None])
    kern = jnp.einsum('dgcp,dkgp,dgpe->dkgce', c_mat, pw[:, :t_len], b_bar).real
    d_eye = jnp.eye(SSM_GROUP, dtype=F32) * d_skip.astype(F32).reshape(SSM_GROUPS, SSM_GROUP, 1)
    t_i = jnp.arange(t_len)
    table = jnp.concatenate([kern[1][:0:-1], (kern[0][0] + kern[1][0] + d_eye)[None], kern[0][1:]], axis=0)
    table = table.transpose(1, 3, 0, 2).reshape(SSM_GROUPS, SSM_GROUP, (2 * t_len - 1) * SSM_GROUP).astype(BF16)
    period = 2 * t_len * SSM_GROUP
    table = jnp.pad(table, ((0, 0), (0, 0), (0, SSM_GROUP)))
    table = jnp.roll(table, -(t_len - 1) * SSM_GROUP, axis=-1)
    flat = jnp.tile(table, (1, 1, t_len))[:, :, :t_len * (period - SSM_GROUP)]
    toep = flat.reshape(SSM_GROUPS, SSM_GROUP, t_len, period - SSM_GROUP)[..., :S5_K]
    toep = toep.transpose(0, 2, 1, 3).reshape(SSM_GROUPS, S5_K, S5_K)
    wf = pw[0][t_len - 1 - t_i][:, :, :, None] * b_bar[0][None]
    wb = pw[1][t_i][:, :, :, None] * b_bar[1][None]

    def state_cols(w):
        w = w.transpose(1, 0, 3, 2).reshape(SSM_GROUPS, S5_K, SSM_STATE)
        return jnp.concatenate([w.real, w.imag, w.imag, w.real], axis=-1)

    w_s = jnp.concatenate([state_cols(wf), state_cols(wb)], axis=-1)
    yf = c_mat[0][None] * pw[0][t_i + 1][:, :, None, :]
    yb = c_mat[1][None] * pw[1][t_len - t_i][:, :, None, :]

    def out_rows(w):
        w = w.transpose(1, 3, 0, 2).reshape(SSM_GROUPS, SSM_STATE, S5_K)
        return jnp.concatenate([w.real, -w.imag], axis=1)

    w_y = jnp.concatenate([out_rows(yf), out_rows(yb)], axis=1)
    a = pw[:, t_len]
    ar, ai = a.real, a.imag

    def coef(d):
        return jnp.stack([jnp.concatenate([ar[d], ar[d]], -1), jnp.concatenate([-ai[d], ai[d]], -1),
                          jnp.concatenate([ai[d], -ai[d]], -1)], axis=1)

    s5_coef = jnp.concatenate([coef(0), coef(1)], axis=-1)
    return toep.astype(BF16), w_s.astype(BF16), w_y.astype(BF16), s5_coef


def _static_mats():
    bd = np.kron(np.eye(GQA_HEADS), np.full((GQA_HEAD_DIM, GQA_HEAD_DIM), 1.0 / GQA_HEAD_DIM))
    half = GQA_HEAD_DIM // 2
    swap = np.zeros((GQA_HEAD_DIM, GQA_HEAD_DIM))
    swap[np.arange(half) + half, np.arange(half)] = 1.0
    swap[np.arange(half), np.arange(half) + half] = 1.0
    perm = np.kron(np.eye(GQA_HEADS), swap)
    place = np.zeros((GQA_WIDTH, GQA_HEADS * SLOT))
    for hd in range(GQA_HEADS):
        kv = hd // (GQA_HEADS // GQA_KV_HEADS)
        for d in range(GQA_HEAD_DIM):
            place[hd * GQA_HEAD_DIM + d, hd * SLOT + kv * GQA_HEAD_DIM + d] = 1.0
    return (jnp.asarray(bd, BF16), jnp.asarray(perm, BF16), jnp.asarray(place, BF16))


def _layer_weights(l, p):
    f = lambda name: p[name][l].astype(F32)
    lw = {}
    w_in = f('w_in')
    u, c_q, c_kv, k_pe, q_g, k_g, v_g = jnp.split(w_in, [256, 512, 640, 672, 1056, 1184], axis=-1)
    z = lambda n: jnp.zeros((D_MODEL, n), F32)
    half = MLA_ROPE // 2
    kpe_slot = jnp.concatenate([z(MLA_NOPE), k_pe, z(32)], axis=-1)
    kpe_rot = jnp.concatenate([z(MLA_NOPE), -k_pe[:, half:], k_pe[:, :half], z(32)], axis=-1)
    lw['w_in'] = jnp.concatenate([u, c_q, c_kv, kpe_slot, kpe_rot, q_g, k_g, v_g], axis=-1).astype(BF16)
    lw['attn_norm'] = f('attn_norm')[None]
    lw['mla_q_norm'] = f('mla_q_norm')[None]
    lw['mla_kv_norm'] = f('mla_kv_norm')[None]
    w_uq = f('mla_w_uq').reshape(MLA_Q_LORA, MLA_HEADS, MLA_NOPE + MLA_ROPE)
    nope, pe = w_uq[..., :MLA_NOPE], w_uq[..., MLA_NOPE:]
    zq = jnp.zeros((MLA_Q_LORA, MLA_HEADS, 32), F32)
    lw['w_uq'] = jnp.concatenate([nope, pe, zq], axis=-1).reshape(MLA_Q_LORA, -1).astype(BF16)
    lw['w_uq_rot'] = jnp.concatenate([jnp.zeros_like(nope), -pe[..., half:], pe[..., :half], zq],
                                     axis=-1).reshape(MLA_Q_LORA, -1).astype(BF16)
    w_ukv = f('mla_w_ukv').reshape(MLA_KV_LORA, MLA_HEADS, MLA_NOPE + MLA_V)
    zk = jnp.zeros((MLA_KV_LORA, MLA_HEADS, SLOT - MLA_NOPE), F32)
    lw['w_ukv_k'] = jnp.concatenate([w_ukv[..., :MLA_NOPE], zk], axis=-1).reshape(MLA_KV_LORA, -1).astype(BF16)
    lw['w_ukv_v'] = w_ukv[..., MLA_NOPE:].reshape(MLA_KV_LORA, -1).astype(BF16)
    lw['gqa_q_norm'] = jnp.tile(f('gqa_q_norm'), GQA_HEADS)[None]
    lw['gqa_k_norm'] = jnp.tile(f('gqa_k_norm'), GQA_KV_HEADS)[None]
    lw['bd'], lw['perm'], lw['place'] = _static_mats()
    lw['s5_toep'], lw['s5_w_s'], lw['s5_w_y'], lw['s5_coef'] = _s5_weights(
        p['s5_lam_re'][l], p['s5_lam_im'][l], p['s5_log_dt'][l], p['s5_b_re'][l], p['s5_b_im'][l],
        p['s5_c_re'][l], p['s5_c_im'][l], p['s5_d'][l])
    lw['w_glu'] = f('s5_w_glu').astype(BF16)
    lw['b_glu'] = f('s5_b_glu')[None]
    lw['out_norm_ssm'] = f('out_norm_ssm')[None]
    lw['out_norm_mla'] = f('out_norm_mla')[None]
    lw['out_norm_gqa'] = f('out_norm_gqa')[None]
    w_out = f('w_out')
    lw['w_out_s'] = w_out[:SSM_WIDTH].astype(BF16)
    lw['w_out_m'] = w_out[SSM_WIDTH:SSM_WIDTH + MLA_WIDTH].astype(BF16)
    lw['w_out_g'] = w_out[SSM_WIDTH + MLA_WIDTH:].astype(BF16)
    lw['ffn_norm'] = f('ffn_norm')[None]
    w_r = jnp.concatenate([f('router_group_w'), jnp.zeros((D_MODEL, 4), F32), f('router_expert_w'),
                           jnp.zeros((D_MODEL, LANE - 8 - N_EXPERTS), F32)], axis=-1)
    lw['w_r_hi'], lw['w_r_lo'] = _split_bf16(w_r)
    lw['r_bias'] = jnp.concatenate([f('router_group_b'), jnp.full((4,), -1e30, F32), f('router_expert_b'),
                                    jnp.zeros((LANE - 8 - N_EXPERTS,), F32)])[None]
    lw['layer'] = l
    lw['w_gate'] = p['expert_w_gate'].astype(F32)
    lw['w_up'] = p['expert_w_up'].astype(F32)
    lw['w_down'] = p['expert_w_down'].astype(F32)
    return lw


def _trunk(x, layers, final_gain):
    bsz, seq_len, _ = x.shape
    tabs = _rope_tables(seq_len)
    xf = x.reshape(bsz * seq_len, D_MODEL).astype(F32)
    depth = len(layers)
    for l, lw in enumerate(layers):
        u, q_m, k_m, v_m, q_g, k_g, v_g = _inproj(xf, lw, tabs, seq_len)
        y = _s5(u, lw, bsz, seq_len)
        o_m = _attention(q_m, k_m, v_m, bsz, seq_len, groups=MLA_HEADS // 2, qslots=((0,), (1,)),
                         kslots=2, kslot=(0, 1), tq=512)
        o_g = _attention(q_g, k_g, v_g, bsz, seq_len, groups=1, qslots=((0, 1, 2), (3, 4, 5)),
                         kslots=1, kslot=(0, 0), tq=256)
        x_mid, h, idx, wts = _outproj(y, o_m, o_g, xf, lw)
        xf = _moe(h, idx, wts, x_mid, lw, final_gain, final=(l == depth - 1))
    return xf.reshape(bsz, seq_len, D_MODEL)


def kernel(x_prompt, x_sample, attn_norm, w_in, s5_lam_re, s5_lam_im, s5_log_dt, s5_b_re, s5_b_im, s5_c_re, s5_c_im, s5_d, s5_w_glu, s5_b_glu, mla_q_norm, mla_w_uq, mla_kv_norm, mla_w_ukv, gqa_q_norm, gqa_k_norm, out_norm_ssm, out_norm_mla, out_norm_gqa, w_out, ffn_norm, router_group_w, router_group_b, router_expert_w, router_expert_b, expert_w_gate, expert_w_up, expert_w_down, final_norm):
    p = dict(attn_norm=attn_norm, w_in=w_in, s5_lam_re=s5_lam_re, s5_lam_im=s5_lam_im, s5_log_dt=s5_log_dt,
             s5_b_re=s5_b_re, s5_b_im=s5_b_im, s5_c_re=s5_c_re, s5_c_im=s5_c_im, s5_d=s5_d,
             s5_w_glu=s5_w_glu, s5_b_glu=s5_b_glu, mla_q_norm=mla_q_norm, mla_w_uq=mla_w_uq,
             mla_kv_norm=mla_kv_norm, mla_w_ukv=mla_w_ukv, gqa_q_norm=gqa_q_norm, gqa_k_norm=gqa_k_norm,
             out_norm_ssm=out_norm_ssm, out_norm_mla=out_norm_mla, out_norm_gqa=out_norm_gqa, w_out=w_out,
             ffn_norm=ffn_norm, router_group_w=router_group_w, router_group_b=router_group_b,
             router_expert_w=router_expert_w, router_expert_b=router_expert_b,
             expert_w_gate=expert_w_gate, expert_w_up=expert_w_up, expert_w_down=expert_w_down)
    depth = w_in.shape[0]
    layers = [_layer_weights(l, p) for l in range(depth)]
    final_gain = final_norm.astype(F32)[None]
    return (_trunk(x_prompt, layers, final_gain), _trunk(x_sample, layers, final_gain))
```

```python
import functools
import math

import jax
import jax.numpy as jnp
import numpy as np
from jax import lax
from jax.experimental import pallas as pl
from jax.experimental.pallas import tpu as pltpu

F32 = jnp.float32
BF16 = jnp.bfloat16

D_MODEL = 1024
GRID_W = 64
NORM_EPS = 1e-6
ROPE_THETA = 10000.0

SSM_WIDTH = 256
SSM_GROUP = 16
SSM_GROUPS = 16
SSM_STATE = 64

MLA_HEADS = 6
MLA_NOPE = 64
MLA_ROPE = 32
MLA_V = 64
MLA_Q_LORA = 256
MLA_KV_LORA = 128
MLA_WIDTH = MLA_HEADS * MLA_V

GQA_HEADS = 6
GQA_KV_HEADS = 2
GQA_HEAD_DIM = 64
GQA_WIDTH = GQA_HEADS * GQA_HEAD_DIM

N_EXPERT_GROUPS = 4
EXPERTS_PER_GROUP = 8
N_EXPERTS = 32
EXPERT_FF = 512

LANE = 128
SUBLANES = 8
SLOT = 128
S5_CHUNK = 64
S5_K = S5_CHUNK * SSM_GROUP
S5_SW = 512
S5_HW = 256
PACK_TOKENS = LANE // SSM_GROUP
PACK_ROWS = S5_CHUNK // PACK_TOKENS
S5_ROWS = 512
S5_CARRY_TB = 4
WIN_PAD = 1536
MOE_TILE = 256
VMEM_LIMIT = 48 * 1024 * 1024
LOG2E = math.log2(math.e)

_C_U, _C_CQ, _C_CKV, _C_KPE, _C_KPER, _C_QG, _C_KG, _C_VG = 0, 256, 512, 640, 768, 896, 1280, 1408


def _cparams(*sem):
    return pltpu.CompilerParams(dimension_semantics=sem, vmem_limit_bytes=VMEM_LIMIT)


def _split_bf16(x):
    hi = x.astype(BF16)
    lo = (x - hi.astype(F32)).astype(BF16)
    return hi, lo


def _dot(a, b):
    return jnp.dot(a, b, preferred_element_type=F32)


def _dot2(x, w):
    hi, lo = _split_bf16(x)
    return _dot(hi, w) + _dot(lo, w)


def _rms(x, gain):
    return x * lax.rsqrt(jnp.mean(x * x, axis=-1, keepdims=True) + NORM_EPS) * gain


def _pack_groups(u, u_ref, pack_ref):
    tm = u.shape[0]
    rows = tm // PACK_TOKENS
    groups_per_tile = LANE // SSM_GROUP
    pack_ref[0] = u[:, 0:LANE]
    pack_ref[1] = u[:, LANE:2 * LANE]
    lane_blk = lax.broadcasted_iota(jnp.int32, (rows, LANE), 1) // SSM_GROUP
    for half in range(SSM_WIDTH // LANE):
        rot = {}
        for tl in range(PACK_TOKENS):
            tok = pack_ref[half, pl.ds(tl, rows, stride=PACK_TOKENS), :]
            for k in range(groups_per_tile):
                rot[tl, k] = tok if k == 0 else pltpu.roll(tok, k * SSM_GROUP, 1)
        for gl in range(groups_per_tile):
            acc = rot[0, (0 - gl) % groups_per_tile]
            for tl in range(1, PACK_TOKENS):
                acc = jnp.where(lane_blk == tl, rot[tl, (tl - gl) % groups_per_tile], acc)
            u_ref[half * groups_per_tile + gl] = acc


def _unpack_groups(y_ref, pack_ref):
    groups_per_tile = LANE // SSM_GROUP
    rows = y_ref.shape[1]
    for tl in range(PACK_TOKENS):
        for half in range(SSM_WIDTH // LANE):
            cols = [y_ref[half * groups_per_tile + gl][:, tl * SSM_GROUP:(tl + 1) * SSM_GROUP]
                    for gl in range(groups_per_tile)]
            pack_ref[half, pl.ds(tl, rows, stride=PACK_TOKENS), :] = jnp.concatenate(cols, axis=-1)
    return jnp.concatenate([pack_ref[0], pack_ref[1]], axis=-1)


def _inproj_kernel(x_ref, g_ref, win_ref, qng_ref, wuq_ref, wuqr_ref, kvng_ref, wukk_ref, wukv_ref,
                   gqn_ref, gkn_ref, bd_ref, perm_ref, place_ref,
                   mc_ref, ms_ref, gc_ref, gs_ref,
                   u_ref, qm_ref, km_ref, vm_ref, qg_ref, kg_ref, vg_ref, pack_ref):
    h = _rms(x_ref[...], g_ref[...]).astype(BF16)
    proj = _dot(h, win_ref[...])
    _pack_groups(proj[:, _C_U:_C_U + SSM_WIDTH], u_ref, pack_ref)
    vg_ref[...] = proj[:, _C_VG:_C_VG + 128].astype(BF16)

    mc = mc_ref[...]
    ms = ms_ref[...]
    cq = _rms(proj[:, _C_CQ:_C_CQ + MLA_Q_LORA], qng_ref[...]).astype(BF16)
    qa = _dot(cq, wuq_ref[...])
    qb = _dot(cq, wuqr_ref[...])
    scale = (MLA_NOPE + MLA_ROPE) ** -0.5 * LOG2E
    for hd in range(MLA_HEADS):
        sl = slice(hd * SLOT, (hd + 1) * SLOT)
        qm_ref[:, sl] = ((qa[:, sl] * mc + qb[:, sl] * ms) * scale).astype(BF16)
    ckv = _rms(proj[:, _C_CKV:_C_CKV + MLA_KV_LORA], kvng_ref[...]).astype(BF16)
    kk = _dot(ckv, wukk_ref[...])
    kpe = proj[:, _C_KPE:_C_KPE + SLOT] * mc + proj[:, _C_KPER:_C_KPER + SLOT] * ms
    for hd in range(MLA_HEADS):
        sl = slice(hd * SLOT, (hd + 1) * SLOT)
        km_ref[:, sl] = (kk[:, sl] + kpe).astype(BF16)
    vm_ref[...] = _dot(ckv, wukv_ref[...]).astype(BF16)

    gc = gc_ref[...]
    gs = gs_ref[...]
    bd = bd_ref[...]
    perm = perm_ref[...]
    qg = proj[:, _C_QG:_C_QG + GQA_WIDTH]
    qn = qg * lax.rsqrt(_dot2(qg * qg, bd) + NORM_EPS) * gqn_ref[...]
    gc3 = jnp.concatenate([gc, gc, gc], axis=-1)
    gs3 = jnp.concatenate([gs, gs, gs], axis=-1)
    qr = (qn * gc3 + _dot2(qn, perm) * gs3) * (GQA_HEAD_DIM ** -0.5 * LOG2E)
    qg_ref[...] = _dot(qr.astype(BF16), place_ref[...]).astype(BF16)
    kg = proj[:, _C_KG:_C_KG + 128]
    kn = kg * lax.rsqrt(_dot2(kg * kg, bd[:128, :128]) + NORM_EPS) * gkn_ref[...]
    kg_ref[...] = (kn * gc + _dot2(kn, perm[:128, :128]) * gs).astype(BF16)


def _inproj(x, lw, tabs, seq_len):
    n_tok = x.shape[0]
    tm = min(512, seq_len)
    nt = n_tok // tm
    per_seq = seq_len // tm

    def row(i):
        return (i, 0)

    def full(i):
        return (0, 0)

    def tab(i):
        return (i % per_seq, 0)

    consts = [lw['attn_norm'], lw['w_in'], lw['mla_q_norm'], lw['w_uq'], lw['w_uq_rot'],
              lw['mla_kv_norm'], lw['w_ukv_k'], lw['w_ukv_v'], lw['gqa_q_norm'], lw['gqa_k_norm'],
              lw['bd'], lw['perm'], lw['place']]
    in_specs = [pl.BlockSpec((tm, D_MODEL), row)]
    in_specs += [pl.BlockSpec(c.shape, full) for c in consts]
    in_specs += [pl.BlockSpec((tm, LANE), tab)] * 4
    widths = [MLA_HEADS * SLOT, MLA_HEADS * SLOT, MLA_WIDTH, GQA_HEADS * SLOT, 128, 128]
    u_spec = pl.BlockSpec((SSM_GROUPS, tm // PACK_TOKENS, LANE), lambda i: (0, i, 0))
    return pl.pallas_call(
        _inproj_kernel,
        grid=(nt,),
        in_specs=in_specs,
        out_specs=[u_spec] + [pl.BlockSpec((tm, w), row) for w in widths],
        out_shape=[jax.ShapeDtypeStruct((SSM_GROUPS, n_tok // PACK_TOKENS, LANE), F32)]
        + [jax.ShapeDtypeStruct((n_tok, w), BF16) for w in widths],
        scratch_shapes=[pltpu.VMEM((SSM_WIDTH // LANE, tm, LANE), F32)],
        compiler_params=_cparams("parallel"),
        name="inproj",
    )(x, *consts, *tabs)


def _chunk_dot(u_ref, w_ref, rows):
    acc = None
    for t_hi in range(PACK_ROWS):
        lhs = u_ref[0, pl.ds(t_hi, rows, stride=PACK_ROWS), :].astype(BF16)
        part = _dot(lhs, w_ref[0, t_hi * LANE:(t_hi + 1) * LANE, :])
        acc = part if acc is None else acc + part
    return acc


def _s5_state_kernel(u_ref, w_ref, s_ref, *, n_chunks, tb):
    res = _chunk_dot(u_ref, w_ref, tb * n_chunks)
    for bb in range(tb):
        s_ref[:, bb * S5_SW:(bb + 1) * S5_SW] = res[bb * n_chunks:(bb + 1) * n_chunks]


def _s5_state(u_t, w_s, n_chunks, tb):
    n_grp = u_t.shape[0]
    rows = u_t.shape[1] // PACK_ROWS
    n_bt = rows // (tb * n_chunks)
    return pl.pallas_call(
        functools.partial(_s5_state_kernel, n_chunks=n_chunks, tb=tb),
        grid=(n_grp, n_bt),
        in_specs=[pl.BlockSpec((1, tb * n_chunks * PACK_ROWS, LANE), lambda g, r: (g, r, 0)),
                  pl.BlockSpec((1, S5_K, S5_SW), lambda g, r: (g, 0, 0))],
        out_specs=pl.BlockSpec((n_chunks, tb * S5_SW), lambda g, r: (0, g * n_bt + r)),
        out_shape=jax.ShapeDtypeStruct((n_chunks, n_grp * n_bt * tb * S5_SW), F32),
        compiler_params=_cparams("parallel", "parallel"),
        name="s5_state",
    )(u_t, w_s)


def _s5_carry_kernel(s_ref, c_ref, h_ref, *, n_chunks, tb):
    a1f, a2f, a3f = c_ref[0, 0:1, 0:LANE], c_ref[0, 1:2, 0:LANE], c_ref[0, 2:3, 0:LANE]
    a1b, a2b, a3b = c_ref[0, 0:1, LANE:], c_ref[0, 1:2, LANE:], c_ref[0, 2:3, LANE:]
    zero = jnp.zeros((1, LANE), F32)
    th = SUBLANES if n_chunks % SUBLANES == 0 else n_chunks
    n_tiles = n_chunks // th
    row = lax.broadcasted_iota(jnp.int32, (th, LANE), 0)

    def body(it, carry):
        base_f = pl.multiple_of(it * th, th)
        base_b = pl.multiple_of((n_tiles - 1 - it) * th, th)
        new = []
        for bb in range(tb):
            hf, hfs, hb, hbs = carry[4 * bb:4 * bb + 4]
            s0 = bb * S5_SW
            h0 = bb * S5_HW
            sf = s_ref[pl.ds(base_f, th), s0:s0 + LANE]
            sfs = s_ref[pl.ds(base_f, th), s0 + LANE:s0 + 2 * LANE]
            sb = s_ref[pl.ds(base_b, th), s0 + 2 * LANE:s0 + 3 * LANE]
            sbs = s_ref[pl.ds(base_b, th), s0 + 3 * LANE:s0 + 4 * LANE]
            out_f = jnp.zeros((th, LANE), F32)
            out_b = jnp.zeros((th, LANE), F32)
            for r in range(th):
                rb = th - 1 - r
                out_f = jnp.where(row == r, hf, out_f)
                out_b = jnp.where(row == rb, hb, out_b)
                hf, hfs = (a1f * hf + a2f * hfs + sf[r:r + 1], a1f * hfs + a3f * hf + sfs[r:r + 1])
                hb, hbs = (a1b * hb + a2b * hbs + sb[rb:rb + 1], a1b * hbs + a3b * hb + sbs[rb:rb + 1])
            h_ref[pl.ds(base_f, th), h0:h0 + LANE] = out_f
            h_ref[pl.ds(base_b, th), h0 + LANE:h0 + 2 * LANE] = out_b
            new += [hf, hfs, hb, hbs]
        return tuple(new)

    lax.fori_loop(0, n_tiles, body, (zero,) * (4 * tb))


def _s5_carry(s, coef, n_chunks, tb):
    n_grp = coef.shape[0]
    n_bt = s.shape[1] // (n_grp * tb * S5_SW)
    return pl.pallas_call(
        functools.partial(_s5_carry_kernel, n_chunks=n_chunks, tb=tb),
        grid=(n_grp, n_bt),
        in_specs=[pl.BlockSpec((n_chunks, tb * S5_SW), lambda g, r: (0, g * n_bt + r)),
                  pl.BlockSpec((1, 3, S5_HW), lambda g, r: (g, 0, 0))],
        out_specs=pl.BlockSpec((n_chunks, tb * S5_HW), lambda g, r: (0, g * n_bt + r)),
        out_shape=jax.ShapeDtypeStruct((n_chunks, n_grp * n_bt * tb * S5_HW), F32),
        compiler_params=_cparams("parallel", "parallel"),
        name="s5_carry",
    )(s, coef)


def _gelu_tanh(y):
    return 0.5 * y * (1.0 + jnp.tanh(math.sqrt(2.0 / math.pi) * (y + 0.044715 * (y * y * y))))


def _s5_out_kernel(u_ref, h_ref, toep_ref, wy_ref, y_ref, *, n_chunks, tb):
    rows = tb * n_chunks
    h = jnp.concatenate([h_ref[:, bb * S5_HW:(bb + 1) * S5_HW] for bb in range(tb)], axis=0)
    y = _gelu_tanh(_chunk_dot(u_ref, toep_ref, rows) + _dot(h.astype(BF16), wy_ref[0]))
    for t_hi in range(PACK_ROWS):
        y_ref[0, pl.ds(t_hi, rows, stride=PACK_ROWS), :] = y[:, t_hi * LANE:(t_hi + 1) * LANE]


def _s5_out(u_t, h_in, toep, w_y, n_chunks, tb):
    n_grp = u_t.shape[0]
    rows = u_t.shape[1] // PACK_ROWS
    n_bt = rows // (tb * n_chunks)
    blk = (1, tb * n_chunks * PACK_ROWS, LANE)
    return pl.pallas_call(
        functools.partial(_s5_out_kernel, n_chunks=n_chunks, tb=tb),
        grid=(n_grp, n_bt),
        in_specs=[pl.BlockSpec(blk, lambda g, r: (g, r, 0)),
                  pl.BlockSpec((n_chunks, tb * S5_HW), lambda g, r: (0, g * n_bt + r)),
                  pl.BlockSpec((1, S5_K, S5_K), lambda g, r: (g, 0, 0)),
                  pl.BlockSpec((1, S5_HW, S5_K), lambda g, r: (g, 0, 0))],
        out_specs=pl.BlockSpec(blk, lambda g, r: (g, r, 0)),
        out_shape=jax.ShapeDtypeStruct(u_t.shape, F32),
        compiler_params=_cparams("parallel", "parallel"),
        name="s5_out",
    )(u_t, h_in, toep, w_y)


def _s5(u, lw, bsz, seq_len):
    n_chunks = seq_len // S5_CHUNK
    tb = max(1, min(bsz, S5_ROWS // n_chunks))
    assert bsz % tb == 0
    s = _s5_state(u, lw['s5_w_s'], n_chunks, tb)
    h_in = _s5_carry(s, lw['s5_coef'], n_chunks, math.gcd(tb, S5_CARRY_TB))
    return _s5_out(u, h_in, lw['s5_toep'], lw['s5_w_y'], n_chunks, tb)


def _attn_kernel(q_ref, k_ref, vt_ref, o_ref, m_ref, a_ref, acc_ref, s_ref, *, qslots, kslot, tq, tk, n_kv):
    n_virt = len(qslots)
    n_qb = q_ref.shape[0] // tq
    m_ref[...] = jnp.full(m_ref.shape, -jnp.inf, F32)
    acc_ref[...] = jnp.zeros(acc_ref.shape, F32)

    def scores(j, qi, ki, first):
        q0 = pl.multiple_of(qi * tq, tq)
        q = jnp.concatenate([q_ref[pl.ds(q0, tq), s * SLOT:(s + 1) * SLOT] for s in qslots[j]], axis=0)
        start = pl.multiple_of(ki * tk, tk)
        kb = k_ref[pl.ds(start, tk), kslot[j] * SLOT:(kslot[j] + 1) * SLOT]
        st = lax.dot_general(kb, q, (((1,), (1,)), ((), ())), preferred_element_type=F32)
        s_ref[j] = st
        m_old = jnp.full(m_ref.shape[1:], -jnp.inf, F32) if first else m_ref[j]
        m_new = jnp.maximum(m_old, jnp.max(st, axis=0, keepdims=True))
        m_ref[j] = m_new
        a_ref[j] = jnp.exp2(m_old - m_new)

    def values(j, ki):
        pt = jnp.exp2(s_ref[j] - m_ref[j]).astype(BF16)
        acc_ref[j] = a_ref[j] * acc_ref[j] + _dot(vt_ref[j, ki], pt)

    def finish(qi):
        pieces = []
        for j in range(n_virt):
            acc = acc_ref[j]
            o = acc[0:64] / acc[64:65]
            for s in range(len(qslots[j])):
                pieces.append(o[:, s * tq:(s + 1) * tq])
        q0 = pl.multiple_of(qi * tq, tq)
        for a in range(len(pieces) // 2):
            pair = jnp.concatenate([pieces[2 * a], pieces[2 * a + 1]], axis=0)
            o_ref[pl.ds(q0, tq), a * SLOT:(a + 1) * SLOT] = pair.T

    scores(0, 0, 0, False)

    if n_qb == 1:
        def body(ki, carry):
            scores(1, 0, ki, False)
            values(0, ki)
            scores(0, 0, ki + 1, False)
            values(1, ki)
            return carry

        lax.fori_loop(0, n_kv - 1, body, 0, unroll=True if n_kv <= 5 else 4)
        scores(1, 0, n_kv - 1, False)
        values(0, n_kv - 1)
        values(1, n_kv - 1)
        finish(0)
    else:
        def qbody(qi, carry):
            for ki in range(n_kv):
                scores(1, qi, ki, ki == 0)
                values(0, ki)
                if ki + 1 < n_kv:
                    scores(0, qi, ki + 1, False)
                else:
                    scores(0, jnp.minimum(qi + 1, n_qb - 1), 0, True)
                values(1, ki)
            finish(qi)
            return carry

        lax.fori_loop(0, n_qb, qbody, 0, unroll=2 if n_qb % 2 == 0 else 1)


V_ROWS = 80
ATTN_STEP_ROWS = 2048


def _attention(q, k, v, bsz, seq_len, *, groups, qslots, kslots, kslot, tq):
    tq = min(tq, seq_len)
    tk = min(512, seq_len)
    n_kv = seq_len // tk
    rows = seq_len if seq_len <= ATTN_STEP_ROWS else tq
    nq = seq_len // rows
    heads = sum(len(s) for s in qslots)
    n_stack = len(qslots[0])
    q3 = q.reshape(bsz, seq_len, -1)
    k3 = k.reshape(bsz, seq_len, -1)
    vt = v.reshape(bsz, n_kv, tk, groups * 2, 64).transpose(0, 3, 1, 4, 2)
    vt = jnp.concatenate([vt, jnp.ones((bsz, groups * 2, n_kv, V_ROWS - 64, tk), BF16)], axis=3)
    ow = heads * 64
    out = pl.pallas_call(
        functools.partial(_attn_kernel, qslots=qslots, kslot=kslot, tq=tq, tk=tk, n_kv=n_kv),
        grid=(bsz, groups, nq),
        in_specs=[pl.BlockSpec((None, rows, heads * SLOT), lambda b, g, i: (b, i, g)),
                  pl.BlockSpec((None, seq_len, kslots * SLOT), lambda b, g, i: (b, 0, g)),
                  pl.BlockSpec((None, 2, n_kv, V_ROWS, tk), lambda b, g, i: (b, g, 0, 0, 0))],
        out_specs=pl.BlockSpec((None, rows, ow), lambda b, g, i: (b, i, g)),
        out_shape=jax.ShapeDtypeStruct((bsz, seq_len, groups * ow), F32),
        scratch_shapes=[pltpu.VMEM((2, 1, n_stack * tq), F32), pltpu.VMEM((2, 1, n_stack * tq), F32),
                        pltpu.VMEM((2, V_ROWS, n_stack * tq), F32), pltpu.VMEM((2, tk, n_stack * tq), F32)],
        compiler_params=_cparams("parallel", "parallel", "arbitrary"),
        name="attention",
    )(q3, k3, vt)
    return out.reshape(bsz * seq_len, groups * ow)


def _outproj_kernel(y_ref, om_ref, og_ref, x_ref, wglu_ref, bglu_ref, ns_ref, nm_ref, ng_ref,
                    wos_ref, wom_ref, wog_ref, fn_ref, wrh_ref, wrl_ref, rb_ref,
                    xo_ref, h_ref, idx_ref, wt_ref, pack_ref):
    y = _unpack_groups(y_ref, pack_ref)
    gate = jax.nn.sigmoid(_dot(y.astype(BF16), wglu_ref[...]) + bglu_ref[...])
    o_s = _rms(y * gate, ns_ref[...]).astype(BF16)
    o_m = _rms(om_ref[...], nm_ref[...]).astype(BF16)
    o_g = _rms(og_ref[...], ng_ref[...]).astype(BF16)
    x_new = x_ref[...] + _dot(o_s, wos_ref[...]) + _dot(o_m, wom_ref[...]) + _dot(o_g, wog_ref[...])
    xo_ref[...] = x_new
    h = _rms(x_new, fn_ref[...])
    h_hi, h_lo = _split_bf16(h)
    h_ref[...] = h_hi
    logits = _dot(h_hi, wrh_ref[...]) + _dot(h_lo, wrh_ref[...]) + _dot(h_hi, wrl_ref[...]) + rb_ref[...]
    lt = logits.T
    row = lax.broadcasted_iota(jnp.int32, (8, lt.shape[1]), 0)

    def first_argmax(v, vmax):
        return jnp.min(jnp.where(v == vmax, row, 8), axis=0, keepdims=True)

    gl = lt[0:8]
    gmax = jnp.max(gl, axis=0, keepdims=True)
    g_w = 1.0 / jnp.sum(jnp.exp(gl - gmax), axis=0, keepdims=True)
    g_idx = first_argmax(gl, gmax)
    el = lt[8:16]
    for g in range(1, N_EXPERT_GROUPS):
        el = jnp.where(g_idx == g, lt[8 + 8 * g:16 + 8 * g], el)
    ee = jnp.exp(el - jnp.max(el, axis=0, keepdims=True))
    ep = ee / jnp.sum(ee, axis=0, keepdims=True)
    p1 = jnp.max(ep, axis=0, keepdims=True)
    i1 = first_argmax(ep, p1)
    ep2 = jnp.where(row == i1, -1.0, ep)
    p2 = jnp.max(ep2, axis=0, keepdims=True)
    i2 = first_argmax(ep2, p2)
    denom = p1 + p2
    idx_ref[0:1, :] = g_idx * EXPERTS_PER_GROUP + i1
    idx_ref[1:2, :] = g_idx * EXPERTS_PER_GROUP + i2
    wt_ref[0:1, :] = g_w * p1 / denom
    wt_ref[1:2, :] = g_w * p2 / denom


def _outproj(y, o_m, o_g, x, lw):
    n_tok = x.shape[0]
    tm = min(512, n_tok)

    def row(i):
        return (i, 0)

    def full(i):
        return (0, 0)

    consts = [lw['w_glu'], lw['b_glu'], lw['out_norm_ssm'], lw['out_norm_mla'], lw['out_norm_gqa'],
              lw['w_out_s'], lw['w_out_m'], lw['w_out_g'], lw['ffn_norm'],
              lw['w_r_hi'], lw['w_r_lo'], lw['r_bias']]
    in_specs = [pl.BlockSpec((SSM_GROUPS, tm // PACK_TOKENS, LANE), lambda i: (0, i, 0)),
                pl.BlockSpec((tm, MLA_WIDTH), row),
                pl.BlockSpec((tm, GQA_WIDTH), row), pl.BlockSpec((tm, D_MODEL), row)]
    in_specs += [pl.BlockSpec(c.shape, full) for c in consts]
    return pl.pallas_call(
        _outproj_kernel,
        grid=(n_tok // tm,),
        in_specs=in_specs,
        out_specs=[pl.BlockSpec((tm, D_MODEL), row), pl.BlockSpec((tm, D_MODEL), row),
                   pl.BlockSpec((2, tm), lambda i: (0, i)), pl.BlockSpec((2, tm), lambda i: (0, i))],
        out_shape=[jax.ShapeDtypeStruct((n_tok, D_MODEL), F32), jax.ShapeDtypeStruct((n_tok, D_MODEL), BF16),
                   jax.ShapeDtypeStruct((2, n_tok), jnp.int32), jax.ShapeDtypeStruct((2, n_tok), F32)],
        scratch_shapes=[pltpu.VMEM((SSM_WIDTH // LANE, tm, LANE), F32)],
        compiler_params=_cparams("parallel"),
        name="outproj_router",
    )(y, o_m, o_g, x, *consts)


def _expert_kernel(be_ref, nu_ref, x_ref, wg_ref, wu_ref, wd_ref, o_ref, wg_s, wu_s, wd_s):
    i = pl.program_id(0)

    @pl.when(jnp.logical_or(i == 0, be_ref[i] != be_ref[jnp.maximum(i - 1, 0)]))
    def _():
        wg_s[...] = wg_ref[0].astype(BF16)
        wu_s[...] = wu_ref[0].astype(BF16)
        wd_s[...] = wd_ref[0].astype(BF16)

    @pl.when(i < nu_ref[0])
    def _():
        xb = x_ref[...]
        hg = _dot(xb, wg_s[...])
        hu = _dot(xb, wu_s[...])
        act = (hg * jax.nn.sigmoid(hg) * hu).astype(BF16)
        o_ref[...] = _dot(act, wd_s[...]).astype(BF16)

    @pl.when(i >= nu_ref[0])
    def _():
        o_ref[...] = jnp.zeros_like(o_ref)


def _experts(buf, block_expert, n_used, layer, w_gate, w_up, w_down):
    rows = buf.shape[0]
    n_blocks = rows // MOE_TILE
    grid_spec = pltpu.PrefetchScalarGridSpec(
        num_scalar_prefetch=2,
        grid=(n_blocks,),
        in_specs=[pl.BlockSpec((MOE_TILE, D_MODEL), lambda i, be, nu: (i, 0)),
                  pl.BlockSpec((None, 1, D_MODEL, EXPERT_FF), lambda i, be, nu: (layer, be[i], 0, 0)),
                  pl.BlockSpec((None, 1, D_MODEL, EXPERT_FF), lambda i, be, nu: (layer, be[i], 0, 0)),
                  pl.BlockSpec((None, 1, EXPERT_FF, D_MODEL), lambda i, be, nu: (layer, be[i], 0, 0))],
        out_specs=pl.BlockSpec((MOE_TILE, D_MODEL), lambda i, be, nu: (i, 0)),
        scratch_shapes=[pltpu.VMEM((D_MODEL, EXPERT_FF), BF16), pltpu.VMEM((D_MODEL, EXPERT_FF), BF16),
                        pltpu.VMEM((EXPERT_FF, D_MODEL), BF16)],
    )
    return pl.pallas_call(
        _expert_kernel,
        grid_spec=grid_spec,
        out_shape=jax.ShapeDtypeStruct((rows, D_MODEL), BF16),
        compiler_params=_cparams("arbitrary"),
        name="experts",
    )(block_expert, n_used, buf, w_gate, w_up, w_down)


def _combine_kernel(x_ref, g0_ref, g1_ref, w_ref, gain_ref, o_ref, *, final):
    w = w_ref[...]
    y = x_ref[...] + w[:, 0:1] * g0_ref[...].astype(F32) + w[:, 1:2] * g1_ref[...].astype(F32)
    if final:
        y = _rms(y, gain_ref[...])
    o_ref[...] = y


def _combine(x, g0, g1, w_col, gain, final):
    n_tok = x.shape[0]
    tm = min(512, n_tok)

    def row(i):
        return (i, 0)

    return pl.pallas_call(
        functools.partial(_combine_kernel, final=final),
        grid=(n_tok // tm,),
        in_specs=[pl.BlockSpec((tm, D_MODEL), row), pl.BlockSpec((tm, D_MODEL), row),
                  pl.BlockSpec((tm, D_MODEL), row), pl.BlockSpec((tm, 2), row),
                  pl.BlockSpec((1, D_MODEL), lambda i: (0, 0))],
        out_specs=pl.BlockSpec((tm, D_MODEL), row),
        out_shape=jax.ShapeDtypeStruct((n_tok, D_MODEL), F32),
        compiler_params=_cparams("parallel"),
        name="moe_combine",
    )(x, g0, g1, w_col, gain)


def _moe(h, idx, wts, x, lw, gain, final):
    n_tok = h.shape[0]
    n_assign = 2 * n_tok
    i32 = jnp.int32
    flat_e = idx.reshape(-1)
    experts = jnp.arange(N_EXPERTS, dtype=i32)
    order = jnp.argsort(flat_e).astype(i32)
    counts = jnp.sum((flat_e[:, None] == experts[None, :]).astype(i32), axis=0)
    padded = (counts + MOE_TILE - 1) // MOE_TILE * MOE_TILE
    end = jnp.cumsum(counts)
    start = end - counts
    pad_end = jnp.cumsum(padded)
    pad_start = pad_end - padded
    n_blocks = -(-n_assign // MOE_TILE) + N_EXPERTS
    rows = n_blocks * MOE_TILE
    blk0 = jnp.arange(n_blocks, dtype=i32) * MOE_TILE
    block_expert = jnp.minimum(jnp.sum((pad_end[None, :] <= blk0[:, None]).astype(i32), axis=1), N_EXPERTS - 1)
    n_used = (pad_end[-1] // MOE_TILE).astype(i32).reshape(1)
    row_e = jnp.repeat(block_expert, MOE_TILE)
    local = jnp.arange(rows, dtype=i32) - pad_start[row_e]
    sorted_idx = jnp.clip(start[row_e] + local, 0, n_assign - 1)
    src_tok = jnp.where(local < counts[row_e], order[sorted_idx] % n_tok, 0)
    sorted_pos = jnp.arange(n_assign, dtype=i32)
    sorted_e = jnp.sum((end[None, :] <= sorted_pos[:, None]).astype(i32), axis=1)
    dest = pad_start[sorted_e] + sorted_pos - start[sorted_e]
    _, pos = lax.sort_key_val(order, dest)
    buf = h.at[src_tok].get(mode='promise_in_bounds')
    out = _experts(buf, block_expert, n_used, lw['layer'], lw['w_gate'], lw['w_up'], lw['w_down'])
    g0 = out.at[pos[:n_tok]].get(mode='promise_in_bounds')
    g1 = out.at[pos[n_tok:]].get(mode='promise_in_bounds')
    return _combine(x, g0, g1, wts.T, gain, final)


def _rope_tables(seq_len):
    t = jnp.arange(seq_len)
    row_idx = (t // GRID_W).astype(F32)
    col_idx = (t % GRID_W).astype(F32)

    def cos_sin(rot_dim):
        n_freq = rot_dim // 4
        inv_freq = ROPE_THETA ** (-jnp.arange(n_freq, dtype=F32) / n_freq)
        ang = jnp.concatenate([row_idx[:, None] * inv_freq, col_idx[:, None] * inv_freq], axis=-1)
        return jnp.cos(ang), jnp.sin(ang)

    cm, sm = cos_sin(MLA_ROPE)
    cg, sg = cos_sin(GQA_HEAD_DIM)
    one = jnp.ones((seq_len, MLA_NOPE), F32)
    zero = jnp.zeros((seq_len, MLA_NOPE), F32)
    pad = jnp.zeros((seq_len, SLOT - MLA_NOPE - MLA_ROPE), F32)
    mla_c = jnp.concatenate([one, cm, cm, pad], axis=-1)
    mla_s = jnp.concatenate([zero, sm, sm, pad], axis=-1)
    gqa_c = jnp.concatenate([cg, cg, cg, cg], axis=-1)
    gqa_s = jnp.concatenate([-sg, sg, -sg, sg], axis=-1)
    return mla_c, mla_s, gqa_c, gqa_s


def _s5_weights(lam_re, lam_im, log_dt, b_re, b_im, c_re, c_im, d_skip):
    t_len = S5_CHUNK
    lam = lax.complex(lam_re.astype(F32), lam_im.astype(F32))
    dt = jnp.exp(log_dt.astype(F32))[..., None]
    lam_dt = lam * dt
    lam_bar = jnp.exp(lam_dt)
    b_bar = ((lam_bar - 1.0) / lam)[..., None] * lax.complex(b_re.astype(F32), b_im.astype(F32))
    c_mat = lax.complex(c_re.astype(F32), c_im.astype(F32))
    k_idx = jnp.arange(t_len + 1, dtype=F32)
    pw = jnp.exp(lam_dt[:, None] * k_idx[None, :, None, None])
    kern = jnp.einsum('dgcp,dkgp,dgpe->dkgce', c_mat, pw[:, :t_len], b_bar).real
    d_eye = jnp.eye(SSM_GROUP, dtype=F32) * d_skip.astype(F32).reshape(SSM_GROUPS, SSM_GROUP, 1)
    t_i = jnp.arange(t_len)
    table = jnp.concatenate([kern[1][:0:-1], (kern[0][0] + kern[1][0] + d_eye)[None], kern[0][1:]], axis=0)
    table = table.transpose(1, 3, 0, 2).reshape(SSM_GROUPS, SSM_GROUP, (2 * t_len - 1) * SSM_GROUP).astype(BF16)
    period = 2 * t_len * SSM_GROUP
    table = jnp.pad(table, ((0, 0), (0, 0), (0, SSM_GROUP)))
    table = jnp.roll(table, -(t_len - 1) * SSM_GROUP, axis=-1)
    flat = jnp.tile(table, (1, 1, t_len))[:, :, :t_len * (period - SSM_GROUP)]
    toep = flat.reshape(SSM_GROUPS, SSM_GROUP, t_len, period - SSM_GROUP)[..., :S5_K]
    toep = toep.transpose(0, 2, 1, 3).reshape(SSM_GROUPS, S5_K, S5_K)
    wf = pw[0][t_len - 1 - t_i][:, :, :, None] * b_bar[0][None]
    wb = pw[1][t_i][:, :, :, None] * b_bar[1][None]

    def state_cols(w):
        w = w.transpose(1, 0, 3, 2).reshape(SSM_GROUPS, S5_K, SSM_STATE)
        return jnp.concatenate([w.real, w.imag, w.imag, w.real], axis=-1)

    w_s = jnp.concatenate([state_cols(wf), state_cols(wb)], axis=-1)
    yf = c_mat[0][None] * pw[0][t_i + 1][:, :, None, :]
    yb = c_mat[1][None] * pw[1][t_len - t_i][:, :, None, :]

    def out_rows(w):
        w = w.transpose(1, 3, 0, 2).reshape(SSM_GROUPS, SSM_STATE, S5_K)
        return jnp.concatenate([w.real, -w.imag], axis=1)

    w_y = jnp.concatenate([out_rows(yf), out_rows(yb)], axis=1)
    a = pw[:, t_len]
    ar, ai = a.real, a.imag

    def coef(d):
        return jnp.stack([jnp.concatenate([ar[d], ar[d]], -1), jnp.concatenate([-ai[d], ai[d]], -1),
                          jnp.concatenate([ai[d], -ai[d]], -1)], axis=1)

    s5_coef = jnp.concatenate([coef(0), coef(1)], axis=-1)
    return toep.astype(BF16), w_s.astype(BF16), w_y.astype(BF16), s5_coef


def _static_mats():
    bd = np.kron(np.eye(GQA_HEADS), np.full((GQA_HEAD_DIM, GQA_HEAD_DIM), 1.0 / GQA_HEAD_DIM))
    half = GQA_HEAD_DIM // 2
    swap = np.zeros((GQA_HEAD_DIM, GQA_HEAD_DIM))
    swap[np.arange(half) + half, np.arange(half)] = 1.0
    swap[np.arange(half), np.arange(half) + half] = 1.0
    perm = np.kron(np.eye(GQA_HEADS), swap)
    place = np.zeros((GQA_WIDTH, GQA_HEADS * SLOT))
    for hd in range(GQA_HEADS):
        kv = hd // (GQA_HEADS // GQA_KV_HEADS)
        for d in range(GQA_HEAD_DIM):
            place[hd * GQA_HEAD_DIM + d, hd * SLOT + kv * GQA_HEAD_DIM + d] = 1.0
    return (jnp.asarray(bd, BF16), jnp.asarray(perm, BF16), jnp.asarray(place, BF16))


def _layer_weights(l, p):
    f = lambda name: p[name][l].astype(F32)
    lw = {}
    w_in = f('w_in')
    u, c_q, c_kv, k_pe, q_g, k_g, v_g = jnp.split(w_in, [256, 512, 640, 672, 1056, 1184], axis=-1)
    z = lambda n: jnp.zeros((D_MODEL, n), F32)
    half = MLA_ROPE // 2
    kpe_slot = jnp.concatenate([z(MLA_NOPE), k_pe, z(32)], axis=-1)
    kpe_rot = jnp.concatenate([z(MLA_NOPE), -k_pe[:, half:], k_pe[:, :half], z(32)], axis=-1)
    lw['w_in'] = jnp.concatenate([u, c_q, c_kv, kpe_slot, kpe_rot, q_g, k_g, v_g], axis=-1).astype(BF16)
    lw['attn_norm'] = f('attn_norm')[None]
    lw['mla_q_norm'] = f('mla_q_norm')[None]
    lw['mla_kv_norm'] = f('mla_kv_norm')[None]
    w_uq = f('mla_w_uq').reshape(MLA_Q_LORA, MLA_HEADS, MLA_NOPE + MLA_ROPE)
    nope, pe = w_uq[..., :MLA_NOPE], w_uq[..., MLA_NOPE:]
    zq = jnp.zeros((MLA_Q_LORA, MLA_HEADS, 32), F32)
    lw['w_uq'] = jnp.concatenate([nope, pe, zq], axis=-1).reshape(MLA_Q_LORA, -1).astype(BF16)
    lw['w_uq_rot'] = jnp.concatenate([jnp.zeros_like(nope), -pe[..., half:], pe[..., :half], zq],
                                     axis=-1).reshape(MLA_Q_LORA, -1).astype(BF16)
    w_ukv = f('mla_w_ukv').reshape(MLA_KV_LORA, MLA_HEADS, MLA_NOPE + MLA_V)
    zk = jnp.zeros((MLA_KV_LORA, MLA_HEADS, SLOT - MLA_NOPE), F32)
    lw['w_ukv_k'] = jnp.concatenate([w_ukv[..., :MLA_NOPE], zk], axis=-1).reshape(MLA_KV_LORA, -1).astype(BF16)
    lw['w_ukv_v'] = w_ukv[..., MLA_NOPE:].reshape(MLA_KV_LORA, -1).astype(BF16)
    lw['gqa_q_norm'] = jnp.tile(f('gqa_q_norm'), GQA_HEADS)[None]
    lw['gqa_k_norm'] = jnp.tile(f('gqa_k_norm'), GQA_KV_HEADS)[None]
    lw['bd'], lw['perm'], lw['place'] = _static_mats()
    lw['s5_toep'], lw['s5_w_s'], lw['s5_w_y'], lw['s5_coef'] = _s5_weights(
        p['s5_lam_re'][l], p['s5_lam_im'][l], p['s5_log_dt'][l], p['s5_b_re'][l], p['s5_b_im'][l],
        p['s5_c_re'][l], p['s5_c_im'][l], p['s5_d'][l])
    lw['w_glu'] = f('s5_w_glu').astype(BF16)
    lw['b_glu'] = f('s5_b_glu')[None]
    lw['out_norm_ssm'] = f('out_norm_ssm')[None]
    lw['out_norm_mla'] = f('out_norm_mla')[None]
    lw['out_norm_gqa'] = f('out_norm_gqa')[None]
    w_out = f('w_out')
    lw['w_out_s'] = w_out[:SSM_WIDTH].astype(BF16)
    lw['w_out_m'] = w_out[SSM_WIDTH:SSM_WIDTH + MLA_WIDTH].astype(BF16)
    lw['w_out_g'] = w_out[SSM_WIDTH + MLA_WIDTH:].astype(BF16)
    lw['ffn_norm'] = f('ffn_norm')[None]
    w_r = jnp.concatenate([f('router_group_w'), jnp.zeros((D_MODEL, 4), F32), f('router_expert_w'),
                           jnp.zeros((D_MODEL, LANE - 8 - N_EXPERTS), F32)], axis=-1)
    lw['w_r_hi'], lw['w_r_lo'] = _split_bf16(w_r)
    lw['r_bias'] = jnp.concatenate([f('router_group_b'), jnp.full((4,), -1e30, F32), f('router_expert_b'),
                                    jnp.zeros((LANE - 8 - N_EXPERTS,), F32)])[None]
    lw['layer'] = l
    lw['w_gate'] = p['expert_w_gate'].astype(F32)
    lw['w_up'] = p['expert_w_up'].astype(F32)
    lw['w_down'] = p['expert_w_down'].astype(F32)
    return lw


def _trunk(x, layers, final_gain):
    bsz, seq_len, _ = x.shape
    tabs = _rope_tables(seq_len)
    xf = x.reshape(bsz * seq_len, D_MODEL).astype(F32)
    depth = len(layers)
    for l, lw in enumerate(layers):
        u, q_m, k_m, v_m, q_g, k_g, v_g = _inproj(xf, lw, tabs, seq_len)
        y = _s5(u, lw, bsz, seq_len)
        o_m = _attention(q_m, k_m, v_m, bsz, seq_len, groups=MLA_HEADS // 2, qslots=((0,), (1,)),
                         kslots=2, kslot=(0, 1), tq=512)
        o_g = _attention(q_g, k_g, v_g, bsz, seq_len, groups=1, qslots=((0, 1, 2), (3, 4, 5)),
                         kslots=1, kslot=(0, 0), tq=256)
        x_mid, h, idx, wts = _outproj(y, o_m, o_g, xf, lw)
        xf = _moe(h, idx, wts, x_mid, lw, final_gain, final=(l == depth - 1))
    return xf.reshape(bsz, seq_len, D_MODEL)


def kernel(x_prompt, x_sample, attn_norm, w_in, s5_lam_re, s5_lam_im, s5_log_dt, s5_b_re, s5_b_im, s5_c_re, s5_c_im, s5_d, s5_w_glu, s5_b_glu, mla_q_norm, mla_w_uq, mla_kv_norm, mla_w_ukv, gqa_q_norm, gqa_k_norm, out_norm_ssm, out_norm_mla, out_norm_gqa, w_out, ffn_norm, router_group_w, router_group_b, router_expert_w, router_expert_b, expert_w_gate, expert_w_up, expert_w_down, final_norm):
    p = dict(attn_norm=attn_norm, w_in=w_in, s5_lam_re=s5_lam_re, s5_lam_im=s5_lam_im, s5_log_dt=s5_log_dt,
             s5_b_re=s5_b_re, s5_b_im=s5_b_im, s5_c_re=s5_c_re, s5_c_im=s5_c_im, s5_d=s5_d,
             s5_w_glu=s5_w_glu, s5_b_glu=s5_b_glu, mla_q_norm=mla_q_norm, mla_w_uq=mla_w_uq,
             mla_kv_norm=mla_kv_norm, mla_w_ukv=mla_w_ukv, gqa_q_norm=gqa_q_norm, gqa_k_norm=gqa_k_norm,
             out_norm_ssm=out_norm_ssm, out_norm_mla=out_norm_mla, out_norm_gqa=out_norm_gqa, w_out=w_out,
             ffn_norm=ffn_norm, router_group_w=router_group_w, router_group_b=router_group_b,
             router_expert_w=router_expert_w, router_expert_b=router_expert_b,
             expert_w_gate=expert_w_gate, expert_w_up=expert_w_up, expert_w_down=expert_w_down)
    depth = w_in.shape[0]
    layers = [_layer_weights(l, p) for l in range(depth)]
    final_gain = final_norm.astype(F32)[None]
    return (_trunk(x_prompt, layers, final_gain), _trunk(x_sample, layers, final_gain))
```

```python
import functools
import math

import jax
import jax.numpy as jnp
import numpy as np
from jax import lax
from jax.experimental import pallas as pl
from jax.experimental.pallas import tpu as pltpu

F32 = jnp.float32
BF16 = jnp.bfloat16

D_MODEL = 1024
GRID_W = 64
NORM_EPS = 1e-6
ROPE_THETA = 10000.0

SSM_WIDTH = 256
SSM_GROUP = 16
SSM_GROUPS = 16
SSM_STATE = 64

MLA_HEADS = 6
MLA_NOPE = 64
MLA_ROPE = 32
MLA_V = 64
MLA_Q_LORA = 256
MLA_KV_LORA = 128
MLA_WIDTH = MLA_HEADS * MLA_V

GQA_HEADS = 6
GQA_KV_HEADS = 2
GQA_HEAD_DIM = 64
GQA_WIDTH = GQA_HEADS * GQA_HEAD_DIM

N_EXPERT_GROUPS = 4
EXPERTS_PER_GROUP = 8
N_EXPERTS = 32
EXPERT_FF = 512

LANE = 128
SUBLANES = 8
SLOT = 128
S5_CHUNK = 64
S5_K = S5_CHUNK * SSM_GROUP
S5_SW = 512
S5_HW = 256
PACK_TOKENS = LANE // SSM_GROUP
PACK_ROWS = S5_CHUNK // PACK_TOKENS
S5_ROWS = 512
S5_CARRY_TB = 4
WIN_PAD = 1408
MOE_TILE = 512
V_ROWS = 80
ATTN_STEP_ROWS = 2048
VMEM_LIMIT = 48 * 1024 * 1024
LOG2E = math.log2(math.e)

_C_U, _C_CQ, _C_CKV, _C_KPE, _C_KPER, _C_QG, _C_KG = 0, 256, 512, 640, 768, 896, 1280


def _cparams(*sem):
    return pltpu.CompilerParams(dimension_semantics=sem, vmem_limit_bytes=VMEM_LIMIT)


def _split_bf16(x):
    hi = x.astype(BF16)
    lo = (x - hi.astype(F32)).astype(BF16)
    return hi, lo


def _dot(a, b):
    return jnp.dot(a, b, preferred_element_type=F32)


def _rms(x, gain):
    return x * lax.rsqrt(jnp.mean(x * x, axis=-1, keepdims=True) + NORM_EPS) * gain


def _pack_groups(u, u_ref, pack_ref):
    tm = u.shape[0]
    rows = tm // PACK_TOKENS
    groups_per_tile = LANE // SSM_GROUP
    pack_ref[0] = u[:, 0:LANE]
    pack_ref[1] = u[:, LANE:2 * LANE]
    lane_blk = lax.broadcasted_iota(jnp.int32, (rows, LANE), 1) // SSM_GROUP
    for half in range(SSM_WIDTH // LANE):
        rot = {}
        for tl in range(PACK_TOKENS):
            tok = pack_ref[half, pl.ds(tl, rows, stride=PACK_TOKENS), :]
            for k in range(groups_per_tile):
                rot[tl, k] = tok if k == 0 else pltpu.roll(tok, k * SSM_GROUP, 1)
        for gl in range(groups_per_tile):
            acc = rot[0, (0 - gl) % groups_per_tile]
            for tl in range(1, PACK_TOKENS):
                acc = jnp.where(lane_blk == tl, rot[tl, (tl - gl) % groups_per_tile], acc)
            u_ref[half * groups_per_tile + gl] = acc


def _unpack_groups(y_ref, pack_ref):
    groups_per_tile = LANE // SSM_GROUP
    rows = y_ref.shape[1]
    for tl in range(PACK_TOKENS):
        for half in range(SSM_WIDTH // LANE):
            cols = [y_ref[half * groups_per_tile + gl][:, tl * SSM_GROUP:(tl + 1) * SSM_GROUP]
                    for gl in range(groups_per_tile)]
            pack_ref[half, pl.ds(tl, rows, stride=PACK_TOKENS), :] = jnp.concatenate(cols, axis=-1)
    return jnp.concatenate([pack_ref[0], pack_ref[1]], axis=-1)


def _dot_nt(a, b):
    return lax.dot_general(a, b, (((1,), (1,)), ((), ())), preferred_element_type=F32)


def _store_values_t(vt, vt_ref):
    for hd in range(vt_ref.shape[0]):
        vt_ref[hd, 0:64, :] = vt[hd * 64:(hd + 1) * 64, :].astype(BF16)
        vt_ref[hd, 64:V_ROWS, :] = jnp.ones((V_ROWS - 64, vt.shape[1]), BF16)


def _inproj_kernel(x_ref, g_ref, win_ref, wvg_ref, qng_ref, wuq_ref, wuqr_ref, kvng_ref, wukk_ref, wukv_ref,
                   gqn_ref, gkn_ref, bd_ref, perm_ref, place_ref,
                   mc_ref, ms_ref, gc_ref, gs_ref,
                   u_ref, qm_ref, km_ref, vm_ref, qg_ref, kg_ref, vg_ref, pack_ref):
    h = _rms(x_ref[...], g_ref[...]).astype(BF16)
    proj = _dot(h, win_ref[...])
    _pack_groups(proj[:, _C_U:_C_U + SSM_WIDTH], u_ref, pack_ref)
    _store_values_t(_dot_nt(wvg_ref[...], h), vg_ref)

    mc = mc_ref[...]
    ms = ms_ref[...]
    cq = _rms(proj[:, _C_CQ:_C_CQ + MLA_Q_LORA], qng_ref[...]).astype(BF16)
    qa = _dot(cq, wuq_ref[...])
    qb = _dot(cq, wuqr_ref[...])
    scale = (MLA_NOPE + MLA_ROPE) ** -0.5 * LOG2E
    for hd in range(MLA_HEADS):
        sl = slice(hd * SLOT, (hd + 1) * SLOT)
        qm_ref[:, sl] = ((qa[:, sl] * mc + qb[:, sl] * ms) * scale).astype(BF16)
    ckv = _rms(proj[:, _C_CKV:_C_CKV + MLA_KV_LORA], kvng_ref[...]).astype(BF16)
    kk = _dot(ckv, wukk_ref[...])
    kpe = proj[:, _C_KPE:_C_KPE + SLOT] * mc + proj[:, _C_KPER:_C_KPER + SLOT] * ms
    for hd in range(MLA_HEADS):
        sl = slice(hd * SLOT, (hd + 1) * SLOT)
        km_ref[:, sl] = (kk[:, sl] + kpe).astype(BF16)
    _store_values_t(_dot_nt(wukv_ref[...], ckv), vm_ref)

    gc = gc_ref[...]
    gs = gs_ref[...]
    bd = bd_ref[...]
    perm = perm_ref[...]
    qg = proj[:, _C_QG:_C_QG + GQA_WIDTH]
    qn = qg * lax.rsqrt(_dot((qg * qg).astype(BF16), bd) + NORM_EPS) * gqn_ref[...]
    gc3 = jnp.concatenate([gc, gc, gc], axis=-1)
    gs3 = jnp.concatenate([gs, gs, gs], axis=-1)
    qr = (qn * gc3 + _dot(qn.astype(BF16), perm) * gs3) * (GQA_HEAD_DIM ** -0.5 * LOG2E)
    qg_ref[...] = _dot(qr.astype(BF16), place_ref[...]).astype(BF16)
    kg = proj[:, _C_KG:_C_KG + 128]
    kn = kg * lax.rsqrt(_dot((kg * kg).astype(BF16), bd[:128, :128]) + NORM_EPS) * gkn_ref[...]
    kg_ref[...] = (kn * gc + _dot(kn.astype(BF16), perm[:128, :128]) * gs).astype(BF16)


def _inproj(x, lw, tabs, seq_len):
    n_tok = x.shape[0]
    tm = min(512, seq_len)
    nt = n_tok // tm
    per_seq = seq_len // tm

    def row(i):
        return (i, 0)

    def full(i):
        return (0, 0)

    def tab(i):
        return (i % per_seq, 0)

    consts = [lw['attn_norm'], lw['w_in'], lw['w_vg_t'], lw['mla_q_norm'], lw['w_uq'], lw['w_uq_rot'],
              lw['mla_kv_norm'], lw['w_ukv_k'], lw['w_ukv_vt'], lw['gqa_q_norm'], lw['gqa_k_norm'],
              lw['bd'], lw['perm'], lw['place']]
    in_specs = [pl.BlockSpec((tm, D_MODEL), row)]
    in_specs += [pl.BlockSpec(c.shape, full) for c in consts]
    in_specs += [pl.BlockSpec((tm, LANE), tab)] * 4
    bsz = n_tok // seq_len
    u_spec = pl.BlockSpec((SSM_GROUPS, tm // PACK_TOKENS, LANE), lambda i: (0, i, 0))

    def vt_spec(n_heads):
        return pl.BlockSpec((None, n_heads, None, V_ROWS, tm), lambda i: (i // per_seq, 0, i % per_seq, 0, 0))

    def vt_shape(n_heads):
        return jax.ShapeDtypeStruct((bsz, n_heads, per_seq, V_ROWS, tm), BF16)

    def tok(w):
        return pl.BlockSpec((tm, w), row), jax.ShapeDtypeStruct((n_tok, w), BF16)

    outs = [(u_spec, jax.ShapeDtypeStruct((SSM_GROUPS, n_tok // PACK_TOKENS, LANE), F32)),
            tok(MLA_HEADS * SLOT), tok(MLA_HEADS * SLOT), (vt_spec(MLA_HEADS), vt_shape(MLA_HEADS)),
            tok(GQA_HEADS * SLOT), tok(128), (vt_spec(GQA_KV_HEADS), vt_shape(GQA_KV_HEADS))]
    return pl.pallas_call(
        _inproj_kernel,
        grid=(nt,),
        in_specs=in_specs,
        out_specs=[o[0] for o in outs],
        out_shape=[o[1] for o in outs],
        scratch_shapes=[pltpu.VMEM((SSM_WIDTH // LANE, tm, LANE), F32)],
        compiler_params=_cparams("parallel"),
        name="inproj",
    )(x, *consts, *tabs)


def _chunk_dot(u_ref, w_ref, rows):
    acc = None
    for t_hi in range(PACK_ROWS):
        lhs = u_ref[0, pl.ds(t_hi, rows, stride=PACK_ROWS), :].astype(BF16)
        part = _dot(lhs, w_ref[0, t_hi * LANE:(t_hi + 1) * LANE, :])
        acc = part if acc is None else acc + part
    return acc


def _s5_state_kernel(u_ref, w_ref, s_ref, *, n_chunks, tb):
    res = _chunk_dot(u_ref, w_ref, tb * n_chunks)
    for bb in range(tb):
        s_ref[:, bb * S5_SW:(bb + 1) * S5_SW] = res[bb * n_chunks:(bb + 1) * n_chunks]


def _s5_state(u_t, w_s, n_chunks, tb):
    n_grp = u_t.shape[0]
    rows = u_t.shape[1] // PACK_ROWS
    n_bt = rows // (tb * n_chunks)
    return pl.pallas_call(
        functools.partial(_s5_state_kernel, n_chunks=n_chunks, tb=tb),
        grid=(n_grp, n_bt),
        in_specs=[pl.BlockSpec((1, tb * n_chunks * PACK_ROWS, LANE), lambda g, r: (g, r, 0)),
                  pl.BlockSpec((1, S5_K, S5_SW), lambda g, r: (g, 0, 0))],
        out_specs=pl.BlockSpec((n_chunks, tb * S5_SW), lambda g, r: (0, g * n_bt + r)),
        out_shape=jax.ShapeDtypeStruct((n_chunks, n_grp * n_bt * tb * S5_SW), F32),
        compiler_params=_cparams("parallel", "parallel"),
        name="s5_state",
    )(u_t, w_s)


def _s5_carry_kernel(s_ref, c_ref, h_ref, *, n_chunks, tb):
    a1f, a2f, a3f = c_ref[0, 0:1, 0:LANE], c_ref[0, 1:2, 0:LANE], c_ref[0, 2:3, 0:LANE]
    a1b, a2b, a3b = c_ref[0, 0:1, LANE:], c_ref[0, 1:2, LANE:], c_ref[0, 2:3, LANE:]
    zero = jnp.zeros((1, LANE), F32)
    th = SUBLANES if n_chunks % SUBLANES == 0 else n_chunks
    n_tiles = n_chunks // th
    row = lax.broadcasted_iota(jnp.int32, (th, LANE), 0)

    def body(it, carry):
        base_f = pl.multiple_of(it * th, th)
        base_b = pl.multiple_of((n_tiles - 1 - it) * th, th)
        new = []
        for bb in range(tb):
            hf, hfs, hb, hbs = carry[4 * bb:4 * bb + 4]
            s0 = bb * S5_SW
            h0 = bb * S5_HW
            sf = s_ref[pl.ds(base_f, th), s0:s0 + LANE]
            sfs = s_ref[pl.ds(base_f, th), s0 + LANE:s0 + 2 * LANE]
            sb = s_ref[pl.ds(base_b, th), s0 + 2 * LANE:s0 + 3 * LANE]
            sbs = s_ref[pl.ds(base_b, th), s0 + 3 * LANE:s0 + 4 * LANE]
            out_f = jnp.zeros((th, LANE), F32)
            out_b = jnp.zeros((th, LANE), F32)
            for r in range(th):
                rb = th - 1 - r
                out_f = jnp.where(row == r, hf, out_f)
                out_b = jnp.where(row == rb, hb, out_b)
                hf, hfs = (a1f * hf + a2f * hfs + sf[r:r + 1], a1f * hfs + a3f * hf + sfs[r:r + 1])
                hb, hbs = (a1b * hb + a2b * hbs + sb[rb:rb + 1], a1b * hbs + a3b * hb + sbs[rb:rb + 1])
            h_ref[pl.ds(base_f, th), h0:h0 + LANE] = out_f
            h_ref[pl.ds(base_b, th), h0 + LANE:h0 + 2 * LANE] = out_b
            new += [hf, hfs, hb, hbs]
        return tuple(new)

    lax.fori_loop(0, n_tiles, body, (zero,) * (4 * tb))


def _s5_carry(s, coef, n_chunks, tb):
    n_grp = coef.shape[0]
    n_bt = s.shape[1] // (n_grp * tb * S5_SW)
    return pl.pallas_call(
        functools.partial(_s5_carry_kernel, n_chunks=n_chunks, tb=tb),
        grid=(n_grp, n_bt),
        in_specs=[pl.BlockSpec((n_chunks, tb * S5_SW), lambda g, r: (0, g * n_bt + r)),
                  pl.BlockSpec((1, 3, S5_HW), lambda g, r: (g, 0, 0))],
        out_specs=pl.BlockSpec((n_chunks, tb * S5_HW), lambda g, r: (0, g * n_bt + r)),
        out_shape=jax.ShapeDtypeStruct((n_chunks, n_grp * n_bt * tb * S5_HW), F32),
        compiler_params=_cparams("parallel", "parallel"),
        name="s5_carry",
    )(s, coef)


def _gelu_tanh(y):
    return 0.5 * y * (1.0 + jnp.tanh(math.sqrt(2.0 / math.pi) * (y + 0.044715 * (y * y * y))))


def _s5_out_kernel(u_ref, h_ref, toep_ref, wy_ref, y_ref, *, n_chunks, tb):
    rows = tb * n_chunks
    h = jnp.concatenate([h_ref[:, bb * S5_HW:(bb + 1) * S5_HW] for bb in range(tb)], axis=0)
    y = _gelu_tanh(_chunk_dot(u_ref, toep_ref, rows) + _dot(h.astype(BF16), wy_ref[0]))
    for t_hi in range(PACK_ROWS):
        y_ref[0, pl.ds(t_hi, rows, stride=PACK_ROWS), :] = y[:, t_hi * LANE:(t_hi + 1) * LANE]


def _s5_out(u_t, h_in, toep, w_y, n_chunks, tb):
    n_grp = u_t.shape[0]
    rows = u_t.shape[1] // PACK_ROWS
    n_bt = rows // (tb * n_chunks)
    blk = (1, tb * n_chunks * PACK_ROWS, LANE)
    return pl.pallas_call(
        functools.partial(_s5_out_kernel, n_chunks=n_chunks, tb=tb),
        grid=(n_grp, n_bt),
        in_specs=[pl.BlockSpec(blk, lambda g, r: (g, r, 0)),
                  pl.BlockSpec((n_chunks, tb * S5_HW), lambda g, r: (0, g * n_bt + r)),
                  pl.BlockSpec((1, S5_K, S5_K), lambda g, r: (g, 0, 0)),
                  pl.BlockSpec((1, S5_HW, S5_K), lambda g, r: (g, 0, 0))],
        out_specs=pl.BlockSpec(blk, lambda g, r: (g, r, 0)),
        out_shape=jax.ShapeDtypeStruct(u_t.shape, F32),
        compiler_params=_cparams("parallel", "parallel"),
        name="s5_out",
    )(u_t, h_in, toep, w_y)


def _s5(u, lw, bsz, seq_len):
    n_chunks = seq_len // S5_CHUNK
    tb = max(1, min(bsz, S5_ROWS // n_chunks))
    assert bsz % tb == 0
    s = _s5_state(u, lw['s5_w_s'], n_chunks, tb)
    h_in = _s5_carry(s, lw['s5_coef'], n_chunks, math.gcd(tb, S5_CARRY_TB))
    return _s5_out(u, h_in, lw['s5_toep'], lw['s5_w_y'], n_chunks, tb)


def _attn_kernel(q_ref, k_ref, vt_ref, o_ref, m_ref, a_ref, acc_ref, s_ref, *, qslots, kslot, tq, tk, n_kv):
    n_virt = len(qslots)
    n_qb = q_ref.shape[0] // tq
    m_ref[...] = jnp.full(m_ref.shape, -jnp.inf, F32)
    acc_ref[...] = jnp.zeros(acc_ref.shape, F32)

    def scores(j, qi, ki, first):
        q0 = pl.multiple_of(qi * tq, tq)
        q = jnp.concatenate([q_ref[pl.ds(q0, tq), s * SLOT:(s + 1) * SLOT] for s in qslots[j]], axis=0)
        start = pl.multiple_of(ki * tk, tk)
        kb = k_ref[pl.ds(start, tk), kslot[j] * SLOT:(kslot[j] + 1) * SLOT]
        st = lax.dot_general(kb, q, (((1,), (1,)), ((), ())), preferred_element_type=F32)
        s_ref[j] = st
        m_old = jnp.full(m_ref.shape[1:], -jnp.inf, F32) if first else m_ref[j]
        m_new = jnp.maximum(m_old, jnp.max(st, axis=0, keepdims=True))
        m_ref[j] = m_new
        a_ref[j] = jnp.exp2(m_old - m_new)

    def values(j, ki):
        pt = jnp.exp2(s_ref[j] - m_ref[j]).astype(BF16)
        acc_ref[j] = a_ref[j] * acc_ref[j] + _dot(vt_ref[j, ki], pt)

    def finish(qi):
        pieces = []
        for j in range(n_virt):
            acc = acc_ref[j]
            o = acc[0:64] / acc[64:65]
            for s in range(len(qslots[j])):
                pieces.append(o[:, s * tq:(s + 1) * tq])
        q0 = pl.multiple_of(qi * tq, tq)
        for a in range(len(pieces) // 2):
            pair = jnp.concatenate([pieces[2 * a], pieces[2 * a + 1]], axis=0)
            o_ref[pl.ds(q0, tq), a * SLOT:(a + 1) * SLOT] = pair.T

    scores(0, 0, 0, False)

    if n_qb == 1:
        def body(ki, carry):
            scores(1, 0, ki, False)
            values(0, ki)
            scores(0, 0, ki + 1, False)
            values(1, ki)
            return carry

        lax.fori_loop(0, n_kv - 1, body, 0, unroll=True if n_kv <= 5 else 4)
        scores(1, 0, n_kv - 1, False)
        values(0, n_kv - 1)
        values(1, n_kv - 1)
        finish(0)
    else:
        def qbody(qi, carry):
            for ki in range(n_kv):
                scores(1, qi, ki, ki == 0)
                values(0, ki)
                if ki + 1 < n_kv:
                    scores(0, qi, ki + 1, False)
                else:
                    scores(0, jnp.minimum(qi + 1, n_qb - 1), 0, True)
                values(1, ki)
            finish(qi)
            return carry

        lax.fori_loop(0, n_qb, qbody, 0, unroll=2 if n_qb % 2 == 0 else 1)


def _attention(q, k, vt, bsz, seq_len, *, groups, qslots, kslots, kslot, tq):
    tq = min(tq, seq_len)
    n_kv, tk = vt.shape[2], vt.shape[4]
    rows = seq_len if seq_len <= ATTN_STEP_ROWS else tq
    nq = seq_len // rows
    heads = sum(len(s) for s in qslots)
    n_stack = len(qslots[0])
    q3 = q.reshape(bsz, seq_len, -1)
    k3 = k.reshape(bsz, seq_len, -1)
    ow = heads * 64
    out = pl.pallas_call(
        functools.partial(_attn_kernel, qslots=qslots, kslot=kslot, tq=tq, tk=tk, n_kv=n_kv),
        grid=(bsz, groups, nq),
        in_specs=[pl.BlockSpec((None, rows, heads * SLOT), lambda b, g, i: (b, i, g)),
                  pl.BlockSpec((None, seq_len, kslots * SLOT), lambda b, g, i: (b, 0, g)),
                  pl.BlockSpec((None, 2, n_kv, V_ROWS, tk), lambda b, g, i: (b, g, 0, 0, 0))],
        out_specs=pl.BlockSpec((None, rows, ow), lambda b, g, i: (b, i, g)),
        out_shape=jax.ShapeDtypeStruct((bsz, seq_len, groups * ow), F32),
        scratch_shapes=[pltpu.VMEM((2, 1, n_stack * tq), F32), pltpu.VMEM((2, 1, n_stack * tq), F32),
                        pltpu.VMEM((2, V_ROWS, n_stack * tq), F32), pltpu.VMEM((2, tk, n_stack * tq), F32)],
        compiler_params=_cparams("parallel", "parallel", "arbitrary"),
        name="attention",
    )(q3, k3, vt)
    return out.reshape(bsz * seq_len, groups * ow)


def _outproj_kernel(y_ref, om_ref, og_ref, x_ref, wglu_ref, bglu_ref, ns_ref, nm_ref, ng_ref,
                    wos_ref, wom_ref, wog_ref, fn_ref, wrh_ref, wrl_ref, rb_ref,
                    xo_ref, h_ref, idx_ref, wt_ref, pack_ref):
    y = _unpack_groups(y_ref, pack_ref)
    gate = jax.nn.sigmoid(_dot(y.astype(BF16), wglu_ref[...]) + bglu_ref[...])
    o_s = _rms(y * gate, ns_ref[...]).astype(BF16)
    o_m = _rms(om_ref[...], nm_ref[...]).astype(BF16)
    o_g = _rms(og_ref[...], ng_ref[...]).astype(BF16)
    x_new = x_ref[...] + _dot(o_s, wos_ref[...]) + _dot(o_m, wom_ref[...]) + _dot(o_g, wog_ref[...])
    xo_ref[...] = x_new
    h = _rms(x_new, fn_ref[...])
    h_hi, h_lo = _split_bf16(h)
    h_ref[...] = h_hi
    logits = _dot(h_hi, wrh_ref[...]) + _dot(h_lo, wrh_ref[...]) + _dot(h_hi, wrl_ref[...]) + rb_ref[...]
    lt = logits.T
    row = lax.broadcasted_iota(jnp.int32, (8, lt.shape[1]), 0)

    def first_argmax(v, vmax):
        return jnp.min(jnp.where(v == vmax, row, 8), axis=0, keepdims=True)

    gl = lt[0:8]
    gmax = jnp.max(gl, axis=0, keepdims=True)
    g_w = 1.0 / jnp.sum(jnp.exp(gl - gmax), axis=0, keepdims=True)
    g_idx = first_argmax(gl, gmax)
    el = lt[8:16]
    for g in range(1, N_EXPERT_GROUPS):
        el = jnp.where(g_idx == g, lt[8 + 8 * g:16 + 8 * g], el)
    ee = jnp.exp(el - jnp.max(el, axis=0, keepdims=True))
    ep = ee / jnp.sum(ee, axis=0, keepdims=True)
    p1 = jnp.max(ep, axis=0, keepdims=True)
    i1 = first_argmax(ep, p1)
    ep2 = jnp.where(row == i1, -1.0, ep)
    p2 = jnp.max(ep2, axis=0, keepdims=True)
    i2 = first_argmax(ep2, p2)
    denom = p1 + p2
    idx_ref[0:1, :] = g_idx * EXPERTS_PER_GROUP + i1
    idx_ref[1:2, :] = g_idx * EXPERTS_PER_GROUP + i2
    wt_ref[0:1, :] = g_w * p1 / denom
    wt_ref[1:2, :] = g_w * p2 / denom


def _outproj(y, o_m, o_g, x, lw):
    n_tok = x.shape[0]
    tm = min(512, n_tok)

    def row(i):
        return (i, 0)

    def full(i):
        return (0, 0)

    consts = [lw['w_glu'], lw['b_glu'], lw['out_norm_ssm'], lw['out_norm_mla'], lw['out_norm_gqa'],
              lw['w_out_s'], lw['w_out_m'], lw['w_out_g'], lw['ffn_norm'],
              lw['w_r_hi'], lw['w_r_lo'], lw['r_bias']]
    in_specs = [pl.BlockSpec((SSM_GROUPS, tm // PACK_TOKENS, LANE), lambda i: (0, i, 0)),
                pl.BlockSpec((tm, MLA_WIDTH), row),
                pl.BlockSpec((tm, GQA_WIDTH), row), pl.BlockSpec((tm, D_MODEL), row)]
    in_specs += [pl.BlockSpec(c.shape, full) for c in consts]
    return pl.pallas_call(
        _outproj_kernel,
        grid=(n_tok // tm,),
        in_specs=in_specs,
        out_specs=[pl.BlockSpec((tm, D_MODEL), row), pl.BlockSpec((tm, D_MODEL), row),
                   pl.BlockSpec((2, tm), lambda i: (0, i)), pl.BlockSpec((2, tm), lambda i: (0, i))],
        out_shape=[jax.ShapeDtypeStruct((n_tok, D_MODEL), F32), jax.ShapeDtypeStruct((n_tok, D_MODEL), BF16),
                   jax.ShapeDtypeStruct((2, n_tok), jnp.int32), jax.ShapeDtypeStruct((2, n_tok), F32)],
        scratch_shapes=[pltpu.VMEM((SSM_WIDTH // LANE, tm, LANE), F32)],
        compiler_params=_cparams("parallel"),
        name="outproj_router",
    )(y, o_m, o_g, x, *consts)


def _expert_kernel(be_ref, nu_ref, x_ref, wg_ref, wu_ref, wd_ref, o_ref, wg_s, wu_s, wd_s):
    i = pl.program_id(0)

    @pl.when(jnp.logical_or(i == 0, be_ref[i] != be_ref[jnp.maximum(i - 1, 0)]))
    def _():
        wg_s[...] = wg_ref[0].astype(BF16)
        wu_s[...] = wu_ref[0].astype(BF16)
        wd_s[...] = wd_ref[0].astype(BF16)

    @pl.when(i < nu_ref[0])
    def _():
        xb = x_ref[...]
        hg = _dot(xb, wg_s[...])
        hu = _dot(xb, wu_s[...])
        act = (hg * jax.nn.sigmoid(hg) * hu).astype(BF16)
        o_ref[...] = _dot(act, wd_s[...]).astype(BF16)

    @pl.when(i >= nu_ref[0])
    def _():
        o_ref[...] = jnp.zeros_like(o_ref)


def _experts(buf, block_expert, n_used, layer, w_gate, w_up, w_down):
    rows = buf.shape[0]
    n_blocks = rows // MOE_TILE
    grid_spec = pltpu.PrefetchScalarGridSpec(
        num_scalar_prefetch=2,
        grid=(n_blocks,),
        in_specs=[pl.BlockSpec((MOE_TILE, D_MODEL), lambda i, be, nu: (i, 0)),
                  pl.BlockSpec((None, 1, D_MODEL, EXPERT_FF), lambda i, be, nu: (layer, be[i], 0, 0)),
                  pl.BlockSpec((None, 1, D_MODEL, EXPERT_FF), lambda i, be, nu: (layer, be[i], 0, 0)),
                  pl.BlockSpec((None, 1, EXPERT_FF, D_MODEL), lambda i, be, nu: (layer, be[i], 0, 0))],
        out_specs=pl.BlockSpec((MOE_TILE, D_MODEL), lambda i, be, nu: (i, 0)),
        scratch_shapes=[pltpu.VMEM((D_MODEL, EXPERT_FF), BF16), pltpu.VMEM((D_MODEL, EXPERT_FF), BF16),
                        pltpu.VMEM((EXPERT_FF, D_MODEL), BF16)],
    )
    return pl.pallas_call(
        _expert_kernel,
        grid_spec=grid_spec,
        out_shape=jax.ShapeDtypeStruct((rows, D_MODEL), BF16),
        compiler_params=_cparams("arbitrary"),
        name="experts",
    )(block_expert, n_used, buf, w_gate, w_up, w_down)


def _combine_kernel(x_ref, g0_ref, g1_ref, w_ref, gain_ref, o_ref, *, final):
    w = w_ref[...]
    y = x_ref[...] + w[:, 0:1] * g0_ref[...].astype(F32) + w[:, 1:2] * g1_ref[...].astype(F32)
    if final:
        y = _rms(y, gain_ref[...])
    o_ref[...] = y


def _combine(x, g0, g1, w_col, gain, final):
    n_tok = x.shape[0]
    tm = min(512, n_tok)

    def row(i):
        return (i, 0)

    return pl.pallas_call(
        functools.partial(_combine_kernel, final=final),
        grid=(n_tok // tm,),
        in_specs=[pl.BlockSpec((tm, D_MODEL), row), pl.BlockSpec((tm, D_MODEL), row),
                  pl.BlockSpec((tm, D_MODEL), row), pl.BlockSpec((tm, 2), row),
                  pl.BlockSpec((1, D_MODEL), lambda i: (0, 0))],
        out_specs=pl.BlockSpec((tm, D_MODEL), row),
        out_shape=jax.ShapeDtypeStruct((n_tok, D_MODEL), F32),
        compiler_params=_cparams("parallel"),
        name="moe_combine",
    )(x, g0, g1, w_col, gain)


def _moe(h, idx, wts, x, lw, gain, final):
    n_tok = h.shape[0]
    n_assign = 2 * n_tok
    i32 = jnp.int32
    flat_e = idx.reshape(-1)
    experts = jnp.arange(N_EXPERTS, dtype=i32)
    order = jnp.argsort(flat_e).astype(i32)
    counts = jnp.sum((flat_e[:, None] == experts[None, :]).astype(i32), axis=0)
    padded = (counts + MOE_TILE - 1) // MOE_TILE * MOE_TILE
    end = jnp.cumsum(counts)
    start = end - counts
    pad_end = jnp.cumsum(padded)
    pad_start = pad_end - padded
    n_blocks = -(-n_assign // MOE_TILE) + N_EXPERTS
    rows = n_blocks * MOE_TILE
    blk0 = jnp.arange(n_blocks, dtype=i32) * MOE_TILE
    block_expert = jnp.minimum(jnp.sum((pad_end[None, :] <= blk0[:, None]).astype(i32), axis=1), N_EXPERTS - 1)
    n_used = (pad_end[-1] // MOE_TILE).astype(i32).reshape(1)
    row_e = jnp.repeat(block_expert, MOE_TILE)
    local = jnp.arange(rows, dtype=i32) - pad_start[row_e]
    sorted_idx = jnp.clip(start[row_e] + local, 0, n_assign - 1)
    src_tok = jnp.where(local < counts[row_e], order[sorted_idx] % n_tok, 0)
    sorted_pos = jnp.arange(n_assign, dtype=i32)
    sorted_e = jnp.sum((end[None, :] <= sorted_pos[:, None]).astype(i32), axis=1)
    dest = pad_start[sorted_e] + sorted_pos - start[sorted_e]
    _, pos = lax.sort_key_val(order, dest)
    buf = h.at[src_tok].get(mode='promise_in_bounds')
    out = _experts(buf, block_expert, n_used, lw['layer'], lw['w_gate'], lw['w_up'], lw['w_down'])
    g0 = out.at[pos[:n_tok]].get(mode='promise_in_bounds')
    g1 = out.at[pos[n_tok:]].get(mode='promise_in_bounds')
    return _combine(x, g0, g1, wts.T, gain, final)


def _rope_tables(seq_len):
    t = jnp.arange(seq_len)
    row_idx = (t // GRID_W).astype(F32)
    col_idx = (t % GRID_W).astype(F32)

    def cos_sin(rot_dim):
        n_freq = rot_dim // 4
        inv_freq = ROPE_THETA ** (-jnp.arange(n_freq, dtype=F32) / n_freq)
        ang = jnp.concatenate([row_idx[:, None] * inv_freq, col_idx[:, None] * inv_freq], axis=-1)
        return jnp.cos(ang), jnp.sin(ang)

    cm, sm = cos_sin(MLA_ROPE)
    cg, sg = cos_sin(GQA_HEAD_DIM)
    one = jnp.ones((seq_len, MLA_NOPE), F32)
    zero = jnp.zeros((seq_len, MLA_NOPE), F32)
    pad = jnp.zeros((seq_len, SLOT - MLA_NOPE - MLA_ROPE), F32)
    mla_c = jnp.concatenate([one, cm, cm, pad], axis=-1)
    mla_s = jnp.concatenate([zero, sm, sm, pad], axis=-1)
    gqa_c = jnp.concatenate([cg, cg, cg, cg], axis=-1)
    gqa_s = jnp.concatenate([-sg, sg, -sg, sg], axis=-1)
    return mla_c, mla_s, gqa_c, gqa_s


def _s5_weights(lam_re, lam_im, log_dt, b_re, b_im, c_re, c_im, d_skip):
    t_len = S5_CHUNK
    lam = lax.complex(lam_re.astype(F32), lam_im.astype(F32))
    dt = jnp.exp(log_dt.astype(F32))[..., None]
    lam_dt = lam * dt
    lam_bar = jnp.exp(lam_dt)
    b_bar = ((lam_bar - 1.0) / lam)[..., None] * lax.complex(b_re.astype(F32), b_im.astype(F32))
    c_mat = lax.complex(c_re.astype(F32), c_im.astype(F32))
    k_idx = jnp.arange(t_len + 1, dtype=F32)
    pw = jnp.exp(lam_dt[:, None] * k_idx[None, :, None, None])
    kern = jnp.einsum('dgcp,dkgp,dgpe->dkgce', c_mat, pw[:, :t_len], b_bar).real
    d_eye = jnp.eye(SSM_GROUP, dtype=F32) * d_skip.astype(F32).reshape(SSM_GROUPS, SSM_GROUP, 1)
    t_i = jnp.arange(t_len)
    table = jnp.concatenate([kern[1][:0:-1], (kern[0][0] + kern[1][0] + d_eye)[None], kern[0][1:]], axis=0)
    table = table.transpose(1, 3, 0, 2).reshape(SSM_GROUPS, SSM_GROUP, (2 * t_len - 1) * SSM_GROUP).astype(BF16)
    period = 2 * t_len * SSM_GROUP
    table = jnp.pad(table, ((0, 0), (0, 0), (0, SSM_GROUP)))
    table = jnp.roll(table, -(t_len - 1) * SSM_GROUP, axis=-1)
    flat = jnp.tile(table, (1, 1, t_len))[:, :, :t_len * (period - SSM_GROUP)]
    toep = flat.reshape(SSM_GROUPS, SSM_GROUP, t_len, period - SSM_GROUP)[..., :S5_K]
    toep = toep.transpose(0, 2, 1, 3).reshape(SSM_GROUPS, S5_K, S5_K)
    wf = pw[0][t_len - 1 - t_i][:, :, :, None] * b_bar[0][None]
    wb = pw[1][t_i][:, :, :, None] * b_bar[1][None]

    def state_cols(w):
        w = w.transpose(1, 0, 3, 2).reshape(SSM_GROUPS, S5_K, SSM_STATE)
        return jnp.concatenate([w.real, w.imag, w.imag, w.real], axis=-1)

    w_s = jnp.concatenate([state_cols(wf), state_cols(wb)], axis=-1)
    yf = c_mat[0][None] * pw[0][t_i + 1][:, :, None, :]
    yb = c_mat[1][None] * pw[1][t_len - t_i][:, :, None, :]

    def out_rows(w):
        w = w.transpose(1, 3, 0, 2).reshape(SSM_GROUPS, SSM_STATE, S5_K)
        return jnp.concatenate([w.real, -w.imag], axis=1)

    w_y = jnp.concatenate([out_rows(yf), out_rows(yb)], axis=1)
    a = pw[:, t_len]
    ar, ai = a.real, a.imag

    def coef(d):
        return jnp.stack([jnp.concatenate([ar[d], ar[d]], -1), jnp.concatenate([-ai[d], ai[d]], -1),
                          jnp.concatenate([ai[d], -ai[d]], -1)], axis=1)

    s5_coef = jnp.concatenate([coef(0), coef(1)], axis=-1)
    return toep.astype(BF16), w_s.astype(BF16), w_y.astype(BF16), s5_coef


def _static_mats():
    bd = np.kron(np.eye(GQA_HEADS), np.full((GQA_HEAD_DIM, GQA_HEAD_DIM), 1.0 / GQA_HEAD_DIM))
    half = GQA_HEAD_DIM // 2
    swap = np.zeros((GQA_HEAD_DIM, GQA_HEAD_DIM))
    swap[np.arange(half) + half, np.arange(half)] = 1.0
    swap[np.arange(half), np.arange(half) + half] = 1.0
    perm = np.kron(np.eye(GQA_HEADS), swap)
    place = np.zeros((GQA_WIDTH, GQA_HEADS * SLOT))
    for hd in range(GQA_HEADS):
        kv = hd // (GQA_HEADS // GQA_KV_HEADS)
        for d in range(GQA_HEAD_DIM):
            place[hd * GQA_HEAD_DIM + d, hd * SLOT + kv * GQA_HEAD_DIM + d] = 1.0
    return (jnp.asarray(bd, BF16), jnp.asarray(perm, BF16), jnp.asarray(place, BF16))


def _layer_weights(l, p):
    f = lambda name: p[name][l].astype(F32)
    lw = {}
    w_in = f('w_in')
    u, c_q, c_kv, k_pe, q_g, k_g, v_g = jnp.split(w_in, [256, 512, 640, 672, 1056, 1184], axis=-1)
    z = lambda n: jnp.zeros((D_MODEL, n), F32)
    half = MLA_ROPE // 2
    kpe_slot = jnp.concatenate([z(MLA_NOPE), k_pe, z(32)], axis=-1)
    kpe_rot = jnp.concatenate([z(MLA_NOPE), -k_pe[:, half:], k_pe[:, :half], z(32)], axis=-1)
    lw['w_in'] = jnp.concatenate([u, c_q, c_kv, kpe_slot, kpe_rot, q_g, k_g], axis=-1).astype(BF16)
    lw['w_vg_t'] = v_g.T.astype(BF16)
    lw['attn_norm'] = f('attn_norm')[None]
    lw['mla_q_norm'] = f('mla_q_norm')[None]
    lw['mla_kv_norm'] = f('mla_kv_norm')[None]
    w_uq = f('mla_w_uq').reshape(MLA_Q_LORA, MLA_HEADS, MLA_NOPE + MLA_ROPE)
    nope, pe = w_uq[..., :MLA_NOPE], w_uq[..., MLA_NOPE:]
    zq = jnp.zeros((MLA_Q_LORA, MLA_HEADS, 32), F32)
    lw['w_uq'] = jnp.concatenate([nope, pe, zq], axis=-1).reshape(MLA_Q_LORA, -1).astype(BF16)
    lw['w_uq_rot'] = jnp.concatenate([jnp.zeros_like(nope), -pe[..., half:], pe[..., :half], zq],
                                     axis=-1).reshape(MLA_Q_LORA, -1).astype(BF16)
    w_ukv = f('mla_w_ukv').reshape(MLA_KV_LORA, MLA_HEADS, MLA_NOPE + MLA_V)
    zk = jnp.zeros((MLA_KV_LORA, MLA_HEADS, SLOT - MLA_NOPE), F32)
    lw['w_ukv_k'] = jnp.concatenate([w_ukv[..., :MLA_NOPE], zk], axis=-1).reshape(MLA_KV_LORA, -1).astype(BF16)
    lw['w_ukv_vt'] = w_ukv[..., MLA_NOPE:].reshape(MLA_KV_LORA, -1).T.astype(BF16)
    lw['gqa_q_norm'] = jnp.tile(f('gqa_q_norm'), GQA_HEADS)[None]
    lw['gqa_k_norm'] = jnp.tile(f('gqa_k_norm'), GQA_KV_HEADS)[None]
    lw['bd'], lw['perm'], lw['place'] = _static_mats()
    lw['s5_toep'], lw['s5_w_s'], lw['s5_w_y'], lw['s5_coef'] = _s5_weights(
        p['s5_lam_re'][l], p['s5_lam_im'][l], p['s5_log_dt'][l], p['s5_b_re'][l], p['s5_b_im'][l],
        p['s5_c_re'][l], p['s5_c_im'][l], p['s5_d'][l])
    lw['w_glu'] = f('s5_w_glu').astype(BF16)
    lw['b_glu'] = f('s5_b_glu')[None]
    lw['out_norm_ssm'] = f('out_norm_ssm')[None]
    lw['out_norm_mla'] = f('out_norm_mla')[None]
    lw['out_norm_gqa'] = f('out_norm_gqa')[None]
    w_out = f('w_out')
    lw['w_out_s'] = w_out[:SSM_WIDTH].astype(BF16)
    lw['w_out_m'] = w_out[SSM_WIDTH:SSM_WIDTH + MLA_WIDTH].astype(BF16)
    lw['w_out_g'] = w_out[SSM_WIDTH + MLA_WIDTH:].astype(BF16)
    lw['ffn_norm'] = f('ffn_norm')[None]
    w_r = jnp.concatenate([f('router_group_w'), jnp.zeros((D_MODEL, 4), F32), f('router_expert_w'),
                           jnp.zeros((D_MODEL, LANE - 8 - N_EXPERTS), F32)], axis=-1)
    lw['w_r_hi'], lw['w_r_lo'] = _split_bf16(w_r)
    lw['r_bias'] = jnp.concatenate([f('router_group_b'), jnp.full((4,), -1e30, F32), f('router_expert_b'),
                                    jnp.zeros((LANE - 8 - N_EXPERTS,), F32)])[None]
    lw['layer'] = l
    lw['w_gate'] = p['expert_w_gate'].astype(F32)
    lw['w_up'] = p['expert_w_up'].astype(F32)
    lw['w_down'] = p['expert_w_down'].astype(F32)
    return lw


def _trunk(x, layers, final_gain):
    bsz, seq_len, _ = x.shape
    tabs = _rope_tables(seq_len)
    xf = x.reshape(bsz * seq_len, D_MODEL).astype(F32)
    depth = len(layers)
    for l, lw in enumerate(layers):
        u, q_m, k_m, v_m, q_g, k_g, v_g = _inproj(xf, lw, tabs, seq_len)
        y = _s5(u, lw, bsz, seq_len)
        o_m = _attention(q_m, k_m, v_m, bsz, seq_len, groups=MLA_HEADS // 2, qslots=((0,), (1,)),
                         kslots=2, kslot=(0, 1), tq=512)
        o_g = _attention(q_g, k_g, v_g, bsz, seq_len, groups=1, qslots=((0, 1, 2), (3, 4, 5)),
                         kslots=1, kslot=(0, 0), tq=256)
        x_mid, h, idx, wts = _outproj(y, o_m, o_g, xf, lw)
        xf = _moe(h, idx, wts, x_mid, lw, final_gain, final=(l == depth - 1))
    return xf.reshape(bsz, seq_len, D_MODEL)


def kernel(x_prompt, x_sample, attn_norm, w_in, s5_lam_re, s5_lam_im, s5_log_dt, s5_b_re, s5_b_im, s5_c_re, s5_c_im, s5_d, s5_w_glu, s5_b_glu, mla_q_norm, mla_w_uq, mla_kv_norm, mla_w_ukv, gqa_q_norm, gqa_k_norm, out_norm_ssm, out_norm_mla, out_norm_gqa, w_out, ffn_norm, router_group_w, router_group_b, router_expert_w, router_expert_b, expert_w_gate, expert_w_up, expert_w_down, final_norm):
    p = dict(attn_norm=attn_norm, w_in=w_in, s5_lam_re=s5_lam_re, s5_lam_im=s5_lam_im, s5_log_dt=s5_log_dt,
             s5_b_re=s5_b_re, s5_b_im=s5_b_im, s5_c_re=s5_c_re, s5_c_im=s5_c_im, s5_d=s5_d,
             s5_w_glu=s5_w_glu, s5_b_glu=s5_b_glu, mla_q_norm=mla_q_norm, mla_w_uq=mla_w_uq,
             mla_kv_norm=mla_kv_norm, mla_w_ukv=mla_w_ukv, gqa_q_norm=gqa_q_norm, gqa_k_norm=gqa_k_norm,
             out_norm_ssm=out_norm_ssm, out_norm_mla=out_norm_mla, out_norm_gqa=out_norm_gqa, w_out=w_out,
             ffn_norm=ffn_norm, router_group_w=router_group_w, router_group_b=router_group_b,
             router_expert_w=router_expert_w, router_expert_b=router_expert_b,
             expert_w_gate=expert_w_gate, expert_w_up=expert_w_up, expert_w_down=expert_w_down)
    depth = w_in.shape[0]
    layers = [_layer_weights(l, p) for l in range(depth)]
    final_gain = final_norm.astype(F32)[None]
    return (_trunk(x_prompt, layers, final_gain), _trunk(x_sample, layers, final_gain))
```

```python
import functools
import math

import jax
import jax.numpy as jnp
import numpy as np
from jax import lax
from jax.experimental import pallas as pl
from jax.experimental.pallas import tpu as pltpu

F32 = jnp.float32
BF16 = jnp.bfloat16

D_MODEL = 1024
GRID_W = 64
NORM_EPS = 1e-6
ROPE_THETA = 10000.0

SSM_WIDTH = 256
SSM_GROUP = 16
SSM_GROUPS = 16
SSM_STATE = 64

MLA_HEADS = 6
MLA_NOPE = 64
MLA_ROPE = 32
MLA_V = 64
MLA_Q_LORA = 256
MLA_KV_LORA = 128
MLA_WIDTH = MLA_HEADS * MLA_V

GQA_HEADS = 6
GQA_KV_HEADS = 2
GQA_HEAD_DIM = 64
GQA_WIDTH = GQA_HEADS * GQA_HEAD_DIM

N_EXPERT_GROUPS = 4
EXPERTS_PER_GROUP = 8
N_EXPERTS = 32
EXPERT_FF = 512

LANE = 128
SUBLANES = 8
SLOT = 128
S5_CHUNK = 64
S5_K = S5_CHUNK * SSM_GROUP
S5_SW = 512
S5_HW = 256
PACK_TOKENS = LANE // SSM_GROUP
PACK_ROWS = S5_CHUNK // PACK_TOKENS
S5_ROWS = 512
S5_CARRY_TB = 4
WIN_PAD = 1408
MOE_TILE = 512
V_ROWS = 80
ATTN_STEP_ROWS = 2048
VMEM_LIMIT = 48 * 1024 * 1024
LOG2E = math.log2(math.e)

_C_U, _C_CQ, _C_CKV, _C_KPE, _C_KPER, _C_QG, _C_KG = 0, 256, 512, 640, 768, 896, 1280


def _cparams(*sem):
    return pltpu.CompilerParams(dimension_semantics=sem, vmem_limit_bytes=VMEM_LIMIT)


def _split_bf16(x):
    hi = x.astype(BF16)
    lo = (x - hi.astype(F32)).astype(BF16)
    return hi, lo


def _dot(a, b):
    return jnp.dot(a, b, preferred_element_type=F32)


def _rms(x, gain):
    return x * lax.rsqrt(jnp.mean(x * x, axis=-1, keepdims=True) + NORM_EPS) * gain


def _pack_groups(u, u_ref, pack_ref):
    tm = u.shape[0]
    rows = tm // PACK_TOKENS
    groups_per_tile = LANE // SSM_GROUP
    pack_ref[0] = u[:, 0:LANE]
    pack_ref[1] = u[:, LANE:2 * LANE]
    lane_blk = lax.broadcasted_iota(jnp.int32, (rows, LANE), 1) // SSM_GROUP
    for half in range(SSM_WIDTH // LANE):
        rot = {}
        for tl in range(PACK_TOKENS):
            tok = pack_ref[half, pl.ds(tl, rows, stride=PACK_TOKENS), :]
            for k in range(groups_per_tile):
                rot[tl, k] = tok if k == 0 else pltpu.roll(tok, k * SSM_GROUP, 1)
        for gl in range(groups_per_tile):
            acc = rot[0, (0 - gl) % groups_per_tile]
            for tl in range(1, PACK_TOKENS):
                acc = jnp.where(lane_blk == tl, rot[tl, (tl - gl) % groups_per_tile], acc)
            u_ref[half * groups_per_tile + gl] = acc


def _unpack_groups(y_ref, pack_ref):
    groups_per_tile = LANE // SSM_GROUP
    rows = y_ref.shape[1]
    for tl in range(PACK_TOKENS):
        for half in range(SSM_WIDTH // LANE):
            cols = [y_ref[half * groups_per_tile + gl][:, tl * SSM_GROUP:(tl + 1) * SSM_GROUP]
                    for gl in range(groups_per_tile)]
            pack_ref[half, pl.ds(tl, rows, stride=PACK_TOKENS), :] = jnp.concatenate(cols, axis=-1)
    return jnp.concatenate([pack_ref[0], pack_ref[1]], axis=-1)


def _dot_nt(a, b):
    return lax.dot_general(a, b, (((1,), (1,)), ((), ())), preferred_element_type=F32)


def _store_values_t(vt, vt_ref):
    for hd in range(vt_ref.shape[0]):
        vt_ref[hd, 0:64, :] = vt[hd * 64:(hd + 1) * 64, :].astype(BF16)
        vt_ref[hd, 64:V_ROWS, :] = jnp.ones((V_ROWS - 64, vt.shape[1]), BF16)


def _inproj_kernel(x_ref, g_ref, win_ref, wvg_ref, qng_ref, wuq_ref, wuqr_ref, kvng_ref, wukk_ref, wukv_ref,
                   gqn_ref, gkn_ref, bd_ref, perm_ref, place_ref,
                   mc_ref, ms_ref, gc_ref, gs_ref,
                   u_ref, qm_ref, km_ref, vm_ref, qg_ref, kg_ref, vg_ref, pack_ref):
    h = _rms(x_ref[...], g_ref[...]).astype(BF16)
    proj = _dot(h, win_ref[...])
    _pack_groups(proj[:, _C_U:_C_U + SSM_WIDTH], u_ref, pack_ref)
    _store_values_t(_dot_nt(wvg_ref[...], h), vg_ref)

    mc = mc_ref[...]
    ms = ms_ref[...]
    cq = _rms(proj[:, _C_CQ:_C_CQ + MLA_Q_LORA], qng_ref[...]).astype(BF16)
    qa = _dot(cq, wuq_ref[...])
    qb = _dot(cq, wuqr_ref[...])
    scale = (MLA_NOPE + MLA_ROPE) ** -0.5 * LOG2E
    for hd in range(MLA_HEADS):
        sl = slice(hd * SLOT, (hd + 1) * SLOT)
        qm_ref[:, sl] = ((qa[:, sl] * mc + qb[:, sl] * ms) * scale).astype(BF16)
    ckv = _rms(proj[:, _C_CKV:_C_CKV + MLA_KV_LORA], kvng_ref[...]).astype(BF16)
    kk = _dot(ckv, wukk_ref[...])
    kpe = proj[:, _C_KPE:_C_KPE + SLOT] * mc + proj[:, _C_KPER:_C_KPER + SLOT] * ms
    for hd in range(MLA_HEADS):
        sl = slice(hd * SLOT, (hd + 1) * SLOT)
        km_ref[:, sl] = (kk[:, sl] + kpe).astype(BF16)
    _store_values_t(_dot_nt(wukv_ref[...], ckv), vm_ref)

    gc = gc_ref[...]
    gs = gs_ref[...]
    bd = bd_ref[...]
    perm = perm_ref[...]
    qg = proj[:, _C_QG:_C_QG + GQA_WIDTH]
    qn = qg * lax.rsqrt(_dot((qg * qg).astype(BF16), bd) + NORM_EPS) * gqn_ref[...]
    gc3 = jnp.concatenate([gc, gc, gc], axis=-1)
    gs3 = jnp.concatenate([gs, gs, gs], axis=-1)
    qr = (qn * gc3 + _dot(qn.astype(BF16), perm) * gs3) * (GQA_HEAD_DIM ** -0.5 * LOG2E)
    qg_ref[...] = _dot(qr.astype(BF16), place_ref[...]).astype(BF16)
    kg = proj[:, _C_KG:_C_KG + 128]
    kn = kg * lax.rsqrt(_dot((kg * kg).astype(BF16), bd[:128, :128]) + NORM_EPS) * gkn_ref[...]
    kg_ref[...] = (kn * gc + _dot(kn.astype(BF16), perm[:128, :128]) * gs).astype(BF16)


def _inproj(x, lw, tabs, seq_len):
    n_tok = x.shape[0]
    tm = min(512, seq_len)
    nt = n_tok // tm
    per_seq = seq_len // tm

    def row(i):
        return (i, 0)

    def full(i):
        return (0, 0)

    def tab(i):
        return (i % per_seq, 0)

    consts = [lw['attn_norm'], lw['w_in'], lw['w_vg_t'], lw['mla_q_norm'], lw['w_uq'], lw['w_uq_rot'],
              lw['mla_kv_norm'], lw['w_ukv_k'], lw['w_ukv_vt'], lw['gqa_q_norm'], lw['gqa_k_norm'],
              lw['bd'], lw['perm'], lw['place']]
    in_specs = [pl.BlockSpec((tm, D_MODEL), row)]
    in_specs += [pl.BlockSpec(c.shape, full) for c in consts]
    in_specs += [pl.BlockSpec((tm, LANE), tab)] * 4
    bsz = n_tok // seq_len
    u_spec = pl.BlockSpec((SSM_GROUPS, tm // PACK_TOKENS, LANE), lambda i: (0, i, 0))

    def vt_spec(n_heads):
        return pl.BlockSpec((None, n_heads, None, V_ROWS, tm), lambda i: (i // per_seq, 0, i % per_seq, 0, 0))

    def vt_shape(n_heads):
        return jax.ShapeDtypeStruct((bsz, n_heads, per_seq, V_ROWS, tm), BF16)

    def tok(w):
        return pl.BlockSpec((tm, w), row), jax.ShapeDtypeStruct((n_tok, w), BF16)

    outs = [(u_spec, jax.ShapeDtypeStruct((SSM_GROUPS, n_tok // PACK_TOKENS, LANE), F32)),
            tok(MLA_HEADS * SLOT), tok(MLA_HEADS * SLOT), (vt_spec(MLA_HEADS), vt_shape(MLA_HEADS)),
            tok(GQA_HEADS * SLOT), tok(128), (vt_spec(GQA_KV_HEADS), vt_shape(GQA_KV_HEADS))]
    return pl.pallas_call(
        _inproj_kernel,
        grid=(nt,),
        in_specs=in_specs,
        out_specs=[o[0] for o in outs],
        out_shape=[o[1] for o in outs],
        scratch_shapes=[pltpu.VMEM((SSM_WIDTH // LANE, tm, LANE), F32)],
        compiler_params=_cparams("parallel"),
        name="inproj",
    )(x, *consts, *tabs)


def _chunk_dot(u_ref, w_ref, rows):
    acc = None
    for t_hi in range(PACK_ROWS):
        lhs = u_ref[0, pl.ds(t_hi, rows, stride=PACK_ROWS), :].astype(BF16)
        part = _dot(lhs, w_ref[0, t_hi * LANE:(t_hi + 1) * LANE, :])
        acc = part if acc is None else acc + part
    return acc


def _s5_state_kernel(u_ref, w_ref, s_ref, *, n_chunks, tb):
    res = _chunk_dot(u_ref, w_ref, tb * n_chunks)
    for bb in range(tb):
        s_ref[:, bb * S5_SW:(bb + 1) * S5_SW] = res[bb * n_chunks:(bb + 1) * n_chunks]


def _s5_state(u_t, w_s, n_chunks, tb):
    n_grp = u_t.shape[0]
    rows = u_t.shape[1] // PACK_ROWS
    n_bt = rows // (tb * n_chunks)
    return pl.pallas_call(
        functools.partial(_s5_state_kernel, n_chunks=n_chunks, tb=tb),
        grid=(n_grp, n_bt),
        in_specs=[pl.BlockSpec((1, tb * n_chunks * PACK_ROWS, LANE), lambda g, r: (g, r, 0)),
                  pl.BlockSpec((1, S5_K, S5_SW), lambda g, r: (g, 0, 0))],
        out_specs=pl.BlockSpec((n_chunks, tb * S5_SW), lambda g, r: (0, g * n_bt + r)),
        out_shape=jax.ShapeDtypeStruct((n_chunks, n_grp * n_bt * tb * S5_SW), F32),
        compiler_params=_cparams("parallel", "parallel"),
        name="s5_state",
    )(u_t, w_s)


def _s5_carry_kernel(s_ref, c_ref, h_ref, *, n_chunks, tb):
    a1f, a2f, a3f = c_ref[0, 0:1, 0:LANE], c_ref[0, 1:2, 0:LANE], c_ref[0, 2:3, 0:LANE]
    a1b, a2b, a3b = c_ref[0, 0:1, LANE:], c_ref[0, 1:2, LANE:], c_ref[0, 2:3, LANE:]
    zero = jnp.zeros((1, LANE), F32)
    th = SUBLANES if n_chunks % SUBLANES == 0 else n_chunks
    n_tiles = n_chunks // th
    row = lax.broadcasted_iota(jnp.int32, (th, LANE), 0)

    def body(it, carry):
        base_f = pl.multiple_of(it * th, th)
        base_b = pl.multiple_of((n_tiles - 1 - it) * th, th)
        new = []
        for bb in range(tb):
            hf, hfs, hb, hbs = carry[4 * bb:4 * bb + 4]
            s0 = bb * S5_SW
            h0 = bb * S5_HW
            sf = s_ref[pl.ds(base_f, th), s0:s0 + LANE]
            sfs = s_ref[pl.ds(base_f, th), s0 + LANE:s0 + 2 * LANE]
            sb = s_ref[pl.ds(base_b, th), s0 + 2 * LANE:s0 + 3 * LANE]
            sbs = s_ref[pl.ds(base_b, th), s0 + 3 * LANE:s0 + 4 * LANE]
            out_f = jnp.zeros((th, LANE), F32)
            out_b = jnp.zeros((th, LANE), F32)
            for r in range(th):
                rb = th - 1 - r
                out_f = jnp.where(row == r, hf, out_f)
                out_b = jnp.where(row == rb, hb, out_b)
                hf, hfs = (a1f * hf + a2f * hfs + sf[r:r + 1], a1f * hfs + a3f * hf + sfs[r:r + 1])
                hb, hbs = (a1b * hb + a2b * hbs + sb[rb:rb + 1], a1b * hbs + a3b * hb + sbs[rb:rb + 1])
            h_ref[pl.ds(base_f, th), h0:h0 + LANE] = out_f
            h_ref[pl.ds(base_b, th), h0 + LANE:h0 + 2 * LANE] = out_b
            new += [hf, hfs, hb, hbs]
        return tuple(new)

    lax.fori_loop(0, n_tiles, body, (zero,) * (4 * tb))


def _s5_carry(s, coef, n_chunks, tb):
    n_grp = coef.shape[0]
    n_bt = s.shape[1] // (n_grp * tb * S5_SW)
    return pl.pallas_call(
        functools.partial(_s5_carry_kernel, n_chunks=n_chunks, tb=tb),
        grid=(n_grp, n_bt),
        in_specs=[pl.BlockSpec((n_chunks, tb * S5_SW), lambda g, r: (0, g * n_bt + r)),
                  pl.BlockSpec((1, 3, S5_HW), lambda g, r: (g, 0, 0))],
        out_specs=pl.BlockSpec((n_chunks, tb * S5_HW), lambda g, r: (0, g * n_bt + r)),
        out_shape=jax.ShapeDtypeStruct((n_chunks, n_grp * n_bt * tb * S5_HW), F32),
        compiler_params=_cparams("parallel", "parallel"),
        name="s5_carry",
    )(s, coef)


def _gelu_tanh(y):
    return 0.5 * y * (1.0 + jnp.tanh(math.sqrt(2.0 / math.pi) * (y + 0.044715 * (y * y * y))))


def _s5_out_kernel(u_ref, h_ref, toep_ref, wy_ref, y_ref, *, n_chunks, tb):
    rows = tb * n_chunks
    h = jnp.concatenate([h_ref[:, bb * S5_HW:(bb + 1) * S5_HW] for bb in range(tb)], axis=0)
    y = _gelu_tanh(_chunk_dot(u_ref, toep_ref, rows) + _dot(h.astype(BF16), wy_ref[0]))
    for t_hi in range(PACK_ROWS):
        y_ref[0, pl.ds(t_hi, rows, stride=PACK_ROWS), :] = y[:, t_hi * LANE:(t_hi + 1) * LANE]


def _s5_out(u_t, h_in, toep, w_y, n_chunks, tb):
    n_grp = u_t.shape[0]
    rows = u_t.shape[1] // PACK_ROWS
    n_bt = rows // (tb * n_chunks)
    blk = (1, tb * n_chunks * PACK_ROWS, LANE)
    return pl.pallas_call(
        functools.partial(_s5_out_kernel, n_chunks=n_chunks, tb=tb),
        grid=(n_grp, n_bt),
        in_specs=[pl.BlockSpec(blk, lambda g, r: (g, r, 0)),
                  pl.BlockSpec((n_chunks, tb * S5_HW), lambda g, r: (0, g * n_bt + r)),
                  pl.BlockSpec((1, S5_K, S5_K), lambda g, r: (g, 0, 0)),
                  pl.BlockSpec((1, S5_HW, S5_K), lambda g, r: (g, 0, 0))],
        out_specs=pl.BlockSpec(blk, lambda g, r: (g, r, 0)),
        out_shape=jax.ShapeDtypeStruct(u_t.shape, F32),
        compiler_params=_cparams("parallel", "parallel"),
        name="s5_out",
    )(u_t, h_in, toep, w_y)


def _s5(u, lw, bsz, seq_len):
    n_chunks = seq_len // S5_CHUNK
    tb = max(1, min(bsz, S5_ROWS // n_chunks))
    assert bsz % tb == 0
    s = _s5_state(u, lw['s5_w_s'], n_chunks, tb)
    h_in = _s5_carry(s, lw['s5_coef'], n_chunks, math.gcd(tb, S5_CARRY_TB))
    return _s5_out(u, h_in, lw['s5_toep'], lw['s5_w_y'], n_chunks, tb)


def _attn_kernel(q_ref, k_ref, vt_ref, o_ref, m_ref, a_ref, acc_ref, s_ref, *, qslots, kslot, tq, tk, n_kv):
    n_virt = len(qslots)
    n_qb = q_ref.shape[0] // tq
    m_ref[...] = jnp.full(m_ref.shape, -jnp.inf, F32)
    acc_ref[...] = jnp.zeros(acc_ref.shape, F32)

    def scores(j, qi, ki, first):
        q0 = pl.multiple_of(qi * tq, tq)
        q = jnp.concatenate([q_ref[pl.ds(q0, tq), s * SLOT:(s + 1) * SLOT] for s in qslots[j]], axis=0)
        start = pl.multiple_of(ki * tk, tk)
        kb = k_ref[pl.ds(start, tk), kslot[j] * SLOT:(kslot[j] + 1) * SLOT]
        st = lax.dot_general(kb, q, (((1,), (1,)), ((), ())), preferred_element_type=F32)
        s_ref[j] = st
        m_old = jnp.full(m_ref.shape[1:], -jnp.inf, F32) if first else m_ref[j]
        m_new = jnp.maximum(m_old, jnp.max(st, axis=0, keepdims=True))
        m_ref[j] = m_new
        a_ref[j] = jnp.exp2(m_old - m_new)

    def values(j, ki):
        pt = jnp.exp2(s_ref[j] - m_ref[j]).astype(BF16)
        acc_ref[j] = a_ref[j] * acc_ref[j] + _dot(vt_ref[j, ki], pt)

    def finish(qi):
        pieces = []
        for j in range(n_virt):
            acc = acc_ref[j]
            o = acc[0:64] / acc[64:65]
            for s in range(len(qslots[j])):
                pieces.append(o[:, s * tq:(s + 1) * tq])
        q0 = pl.multiple_of(qi * tq, tq)
        for a in range(len(pieces) // 2):
            pair = jnp.concatenate([pieces[2 * a], pieces[2 * a + 1]], axis=0)
            o_ref[pl.ds(q0, tq), a * SLOT:(a + 1) * SLOT] = pair.T

    scores(0, 0, 0, False)

    if n_qb == 1:
        def body(ki, carry):
            scores(1, 0, ki, False)
            values(0, ki)
            scores(0, 0, ki + 1, False)
            values(1, ki)
            return carry

        lax.fori_loop(0, n_kv - 1, body, 0, unroll=True if n_kv <= 5 else 4)
        scores(1, 0, n_kv - 1, False)
        values(0, n_kv - 1)
        values(1, n_kv - 1)
        finish(0)
    else:
        def qbody(qi, carry):
            for ki in range(n_kv):
                scores(1, qi, ki, ki == 0)
                values(0, ki)
                if ki + 1 < n_kv:
                    scores(0, qi, ki + 1, False)
                else:
                    scores(0, jnp.minimum(qi + 1, n_qb - 1), 0, True)
                values(1, ki)
            finish(qi)
            return carry

        lax.fori_loop(0, n_qb, qbody, 0, unroll=2 if n_qb % 2 == 0 else 1)


def _attention(q, k, vt, bsz, seq_len, *, groups, qslots, kslots, kslot, tq):
    tq = min(tq, seq_len)
    n_kv, tk = vt.shape[2], vt.shape[4]
    rows = seq_len if seq_len <= ATTN_STEP_ROWS else tq
    nq = seq_len // rows
    heads = sum(len(s) for s in qslots)
    n_stack = len(qslots[0])
    q3 = q.reshape(bsz, seq_len, -1)
    k3 = k.reshape(bsz, seq_len, -1)
    ow = heads * 64
    out = pl.pallas_call(
        functools.partial(_attn_kernel, qslots=qslots, kslot=kslot, tq=tq, tk=tk, n_kv=n_kv),
        grid=(bsz, groups, nq),
        in_specs=[pl.BlockSpec((None, rows, heads * SLOT), lambda b, g, i: (b, i, g)),
                  pl.BlockSpec((None, seq_len, kslots * SLOT), lambda b, g, i: (b, 0, g)),
                  pl.BlockSpec((None, 2, n_kv, V_ROWS, tk), lambda b, g, i: (b, g, 0, 0, 0))],
        out_specs=pl.BlockSpec((None, rows, ow), lambda b, g, i: (b, i, g)),
        out_shape=jax.ShapeDtypeStruct((bsz, seq_len, groups * ow), F32),
        scratch_shapes=[pltpu.VMEM((2, 1, n_stack * tq), F32), pltpu.VMEM((2, 1, n_stack * tq), F32),
                        pltpu.VMEM((2, V_ROWS, n_stack * tq), F32), pltpu.VMEM((2, tk, n_stack * tq), F32)],
        compiler_params=_cparams("parallel", "parallel", "arbitrary"),
        name="attention",
    )(q3, k3, vt)
    return out.reshape(bsz * seq_len, groups * ow)


def _outproj_kernel(y_ref, om_ref, og_ref, x_ref, wglu_ref, bglu_ref, ns_ref, nm_ref, ng_ref,
                    wos_ref, wom_ref, wog_ref, fn_ref, wrh_ref, wrl_ref, rb_ref,
                    xo_ref, h_ref, idx_ref, wt_ref, pack_ref):
    y = _unpack_groups(y_ref, pack_ref)
    gate = jax.nn.sigmoid(_dot(y.astype(BF16), wglu_ref[...]) + bglu_ref[...])
    o_s = _rms(y * gate, ns_ref[...]).astype(BF16)
    o_m = _rms(om_ref[...], nm_ref[...]).astype(BF16)
    o_g = _rms(og_ref[...], ng_ref[...]).astype(BF16)
    x_new = x_ref[...] + _dot(o_s, wos_ref[...]) + _dot(o_m, wom_ref[...]) + _dot(o_g, wog_ref[...])
    xo_ref[...] = x_new
    h = _rms(x_new, fn_ref[...])
    h_hi, h_lo = _split_bf16(h)
    h_ref[...] = h_hi
    logits = _dot(h_hi, wrh_ref[...]) + _dot(h_lo, wrh_ref[...]) + _dot(h_hi, wrl_ref[...]) + rb_ref[...]
    lt = logits.T
    row = lax.broadcasted_iota(jnp.int32, (8, lt.shape[1]), 0)

    def first_argmax(v, vmax):
        return jnp.min(jnp.where(v == vmax, row, 8), axis=0, keepdims=True)

    gl = lt[0:8]
    gmax = jnp.max(gl, axis=0, keepdims=True)
    g_w = 1.0 / jnp.sum(jnp.exp(gl - gmax), axis=0, keepdims=True)
    g_idx = first_argmax(gl, gmax)
    el = lt[8:16]
    for g in range(1, N_EXPERT_GROUPS):
        el = jnp.where(g_idx == g, lt[8 + 8 * g:16 + 8 * g], el)
    ee = jnp.exp(el - jnp.max(el, axis=0, keepdims=True))
    ep = ee / jnp.sum(ee, axis=0, keepdims=True)
    p1 = jnp.max(ep, axis=0, keepdims=True)
    i1 = first_argmax(ep, p1)
    ep2 = jnp.where(row == i1, -1.0, ep)
    p2 = jnp.max(ep2, axis=0, keepdims=True)
    i2 = first_argmax(ep2, p2)
    denom = p1 + p2
    idx_ref[0:1, :] = g_idx * EXPERTS_PER_GROUP + i1
    idx_ref[1:2, :] = g_idx * EXPERTS_PER_GROUP + i2
    wt_ref[0:1, :] = g_w * p1 / denom
    wt_ref[1:2, :] = g_w * p2 / denom


def _outproj(y, o_m, o_g, x, lw):
    n_tok = x.shape[0]
    tm = min(512, n_tok)

    def row(i):
        return (i, 0)

    def full(i):
        return (0, 0)

    consts = [lw['w_glu'], lw['b_glu'], lw['out_norm_ssm'], lw['out_norm_mla'], lw['out_norm_gqa'],
              lw['w_out_s'], lw['w_out_m'], lw['w_out_g'], lw['ffn_norm'],
              lw['w_r_hi'], lw['w_r_lo'], lw['r_bias']]
    in_specs = [pl.BlockSpec((SSM_GROUPS, tm // PACK_TOKENS, LANE), lambda i: (0, i, 0)),
                pl.BlockSpec((tm, MLA_WIDTH), row),
                pl.BlockSpec((tm, GQA_WIDTH), row), pl.BlockSpec((tm, D_MODEL), row)]
    in_specs += [pl.BlockSpec(c.shape, full) for c in consts]
    return pl.pallas_call(
        _outproj_kernel,
        grid=(n_tok // tm,),
        in_specs=in_specs,
        out_specs=[pl.BlockSpec((tm, D_MODEL), row), pl.BlockSpec((tm, D_MODEL), row),
                   pl.BlockSpec((2, tm), lambda i: (0, i)), pl.BlockSpec((2, tm), lambda i: (0, i))],
        out_shape=[jax.ShapeDtypeStruct((n_tok, D_MODEL), F32), jax.ShapeDtypeStruct((n_tok, D_MODEL), BF16),
                   jax.ShapeDtypeStruct((2, n_tok), jnp.int32), jax.ShapeDtypeStruct((2, n_tok), F32)],
        scratch_shapes=[pltpu.VMEM((SSM_WIDTH // LANE, tm, LANE), F32)],
        compiler_params=_cparams("parallel"),
        name="outproj_router",
    )(y, o_m, o_g, x, *consts)


def _expert_kernel(be_ref, nu_ref, x_ref, wg_ref, wu_ref, wd_ref, o_ref, wg_s, wu_s, wd_s):
    i = pl.program_id(0)

    @pl.when(jnp.logical_or(i == 0, be_ref[i] != be_ref[jnp.maximum(i - 1, 0)]))
    def _():
        wg_s[...] = wg_ref[0].astype(BF16)
        wu_s[...] = wu_ref[0].astype(BF16)
        wd_s[...] = wd_ref[0].astype(BF16)

    @pl.when(i < nu_ref[0])
    def _():
        xb = x_ref[...]
        hg = _dot(xb, wg_s[...])
        hu = _dot(xb, wu_s[...])
        act = (hg * jax.nn.sigmoid(hg) * hu).astype(BF16)
        o_ref[...] = _dot(act, wd_s[...]).astype(BF16)

    @pl.when(i >= nu_ref[0])
    def _():
        o_ref[...] = jnp.zeros_like(o_ref)


def _experts(buf, block_expert, n_used, layer, w_gate, w_up, w_down):
    rows = buf.shape[0]
    n_blocks = rows // MOE_TILE
    grid_spec = pltpu.PrefetchScalarGridSpec(
        num_scalar_prefetch=2,
        grid=(n_blocks,),
        in_specs=[pl.BlockSpec((MOE_TILE, D_MODEL), lambda i, be, nu: (i, 0)),
                  pl.BlockSpec((None, 1, D_MODEL, EXPERT_FF), lambda i, be, nu: (layer, be[i], 0, 0)),
                  pl.BlockSpec((None, 1, D_MODEL, EXPERT_FF), lambda i, be, nu: (layer, be[i], 0, 0)),
                  pl.BlockSpec((None, 1, EXPERT_FF, D_MODEL), lambda i, be, nu: (layer, be[i], 0, 0))],
        out_specs=pl.BlockSpec((MOE_TILE, D_MODEL), lambda i, be, nu: (i, 0)),
        scratch_shapes=[pltpu.VMEM((D_MODEL, EXPERT_FF), BF16), pltpu.VMEM((D_MODEL, EXPERT_FF), BF16),
                        pltpu.VMEM((EXPERT_FF, D_MODEL), BF16)],
    )
    return pl.pallas_call(
        _expert_kernel,
        grid_spec=grid_spec,
        out_shape=jax.ShapeDtypeStruct((rows, D_MODEL), BF16),
        compiler_params=_cparams("arbitrary"),
        name="experts",
    )(block_expert, n_used, buf, w_gate, w_up, w_down)


def _combine_kernel(x_ref, g0_ref, g1_ref, w_ref, gain_ref, o_ref, *, final):
    w = w_ref[...]
    y = x_ref[...] + w[:, 0:1] * g0_ref[...].astype(F32) + w[:, 1:2] * g1_ref[...].astype(F32)
    if final:
        y = _rms(y, gain_ref[...])
    o_ref[...] = y


def _combine(x, g0, g1, w_col, gain, final):
    n_tok = x.shape[0]
    tm = min(512, n_tok)

    def row(i):
        return (i, 0)

    return pl.pallas_call(
        functools.partial(_combine_kernel, final=final),
        grid=(n_tok // tm,),
        in_specs=[pl.BlockSpec((tm, D_MODEL), row), pl.BlockSpec((tm, D_MODEL), row),
                  pl.BlockSpec((tm, D_MODEL), row), pl.BlockSpec((tm, 2), row),
                  pl.BlockSpec((1, D_MODEL), lambda i: (0, 0))],
        out_specs=pl.BlockSpec((tm, D_MODEL), row),
        out_shape=jax.ShapeDtypeStruct((n_tok, D_MODEL), F32),
        compiler_params=_cparams("parallel"),
        name="moe_combine",
    )(x, g0, g1, w_col, gain)


def _moe(h, idx, wts, x, lw, gain, final):
    n_tok = h.shape[0]
    n_assign = 2 * n_tok
    i32 = jnp.int32
    flat_e = idx.reshape(-1)
    experts = jnp.arange(N_EXPERTS, dtype=i32)
    order = jnp.argsort(flat_e).astype(i32)
    counts = jnp.sum((flat_e[:, None] == experts[None, :]).astype(i32), axis=0)
    padded = (counts + MOE_TILE - 1) // MOE_TILE * MOE_TILE
    end = jnp.cumsum(counts)
    start = end - counts
    pad_end = jnp.cumsum(padded)
    pad_start = pad_end - padded
    n_blocks = -(-n_assign // MOE_TILE) + N_EXPERTS
    rows = n_blocks * MOE_TILE
    blk0 = jnp.arange(n_blocks, dtype=i32) * MOE_TILE
    block_expert = jnp.minimum(jnp.sum((pad_end[None, :] <= blk0[:, None]).astype(i32), axis=1), N_EXPERTS - 1)
    n_used = (pad_end[-1] // MOE_TILE).astype(i32).reshape(1)
    row_e = jnp.repeat(block_expert, MOE_TILE)
    local = jnp.arange(rows, dtype=i32) - pad_start[row_e]
    sorted_idx = jnp.clip(start[row_e] + local, 0, n_assign - 1)
    src_tok = jnp.where(local < counts[row_e], order[sorted_idx] % n_tok, 0)
    sorted_pos = jnp.arange(n_assign, dtype=i32)
    sorted_e = jnp.sum((end[None, :] <= sorted_pos[:, None]).astype(i32), axis=1)
    dest = pad_start[sorted_e] + sorted_pos - start[sorted_e]
    _, pos = lax.sort_key_val(order, dest)
    buf = h.at[src_tok].get(mode='promise_in_bounds')
    out = _experts(buf, block_expert, n_used, lw['layer'], lw['w_gate'], lw['w_up'], lw['w_down'])
    g0 = out.at[pos[:n_tok]].get(mode='promise_in_bounds')
    g1 = out.at[pos[n_tok:]].get(mode='promise_in_bounds')
    return _combine(x, g0, g1, wts.T, gain, final)


def _rope_tables(seq_len):
    t = jnp.arange(seq_len)
    row_idx = (t // GRID_W).astype(F32)
    col_idx = (t % GRID_W).astype(F32)

    def cos_sin(rot_dim):
        n_freq = rot_dim // 4
        inv_freq = ROPE_THETA ** (-jnp.arange(n_freq, dtype=F32) / n_freq)
        ang = jnp.concatenate([row_idx[:, None] * inv_freq, col_idx[:, None] * inv_freq], axis=-1)
        return jnp.cos(ang), jnp.sin(ang)

    cm, sm = cos_sin(MLA_ROPE)
    cg, sg = cos_sin(GQA_HEAD_DIM)
    one = jnp.ones((seq_len, MLA_NOPE), F32)
    zero = jnp.zeros((seq_len, MLA_NOPE), F32)
    pad = jnp.zeros((seq_len, SLOT - MLA_NOPE - MLA_ROPE), F32)
    mla_c = jnp.concatenate([one, cm, cm, pad], axis=-1)
    mla_s = jnp.concatenate([zero, sm, sm, pad], axis=-1)
    gqa_c = jnp.concatenate([cg, cg, cg, cg], axis=-1)
    gqa_s = jnp.concatenate([-sg, sg, -sg, sg], axis=-1)
    return mla_c, mla_s, gqa_c, gqa_s


def _s5_weights(lam_re, lam_im, log_dt, b_re, b_im, c_re, c_im, d_skip):
    t_len = S5_CHUNK
    lam = lax.complex(lam_re.astype(F32), lam_im.astype(F32))
    dt = jnp.exp(log_dt.astype(F32))[..., None]
    lam_dt = lam * dt
    lam_bar = jnp.exp(lam_dt)
    b_bar = ((lam_bar - 1.0) / lam)[..., None] * lax.complex(b_re.astype(F32), b_im.astype(F32))
    c_mat = lax.complex(c_re.astype(F32), c_im.astype(F32))
    k_idx = jnp.arange(t_len + 1, dtype=F32)
    pw = jnp.exp(lam_dt[:, None] * k_idx[None, :, None, None])
    kern = jnp.einsum('dgcp,dkgp,dgpe->dkgce', c_mat, pw[:, :t_len], b_bar).real
    d_eye = jnp.eye(SSM_GROUP, dtype=F32) * d_skip.astype(F32).reshape(SSM_GROUPS, SSM_GROUP, 1)
    t_i = jnp.arange(t_len)
    table = jnp.concatenate([kern[1][:0:-1], (kern[0][0] + kern[1][0] + d_eye)[None], kern[0][1:]], axis=0)
    table = table.transpose(1, 3, 0, 2).reshape(SSM_GROUPS, SSM_GROUP, (2 * t_len - 1) * SSM_GROUP).astype(BF16)
    period = 2 * t_len * SSM_GROUP
    table = jnp.pad(table, ((0, 0), (0, 0), (0, SSM_GROUP)))
    table = jnp.roll(table, -(t_len - 1) * SSM_GROUP, axis=-1)
    flat = jnp.tile(table, (1, 1, t_len))[:, :, :t_len * (period - SSM_GROUP)]
    toep = flat.reshape(SSM_GROUPS, SSM_GROUP, t_len, period - SSM_GROUP)[..., :S5_K]
    toep = toep.transpose(0, 2, 1, 3).reshape(SSM_GROUPS, S5_K, S5_K)
    wf = pw[0][t_len - 1 - t_i][:, :, :, None] * b_bar[0][None]
    wb = pw[1][t_i][:, :, :, None] * b_bar[1][None]

    def state_cols(w):
        w = w.transpose(1, 0, 3, 2).reshape(SSM_GROUPS, S5_K, SSM_STATE)
        return jnp.concatenate([w.real, w.imag, w.imag, w.real], axis=-1)

    w_s = jnp.concatenate([state_cols(wf), state_cols(wb)], axis=-1)
    yf = c_mat[0][None] * pw[0][t_i + 1][:, :, None, :]
    yb = c_mat[1][None] * pw[1][t_len - t_i][:, :, None, :]

    def out_rows(w):
        w = w.transpose(1, 3, 0, 2).reshape(SSM_GROUPS, SSM_STATE, S5_K)
        return jnp.concatenate([w.real, -w.imag], axis=1)

    w_y = jnp.concatenate([out_rows(yf), out_rows(yb)], axis=1)
    a = pw[:, t_len]
    ar, ai = a.real, a.imag

    def coef(d):
        return jnp.stack([jnp.concatenate([ar[d], ar[d]], -1), jnp.concatenate([-ai[d], ai[d]], -1),
                          jnp.concatenate([ai[d], -ai[d]], -1)], axis=1)

    s5_coef = jnp.concatenate([coef(0), coef(1)], axis=-1)
    return toep.astype(BF16), w_s.astype(BF16), w_y.astype(BF16), s5_coef


def _static_mats():
    bd = np.kron(np.eye(GQA_HEADS), np.full((GQA_HEAD_DIM, GQA_HEAD_DIM), 1.0 / GQA_HEAD_DIM))
    half = GQA_HEAD_DIM // 2
    swap = np.zeros((GQA_HEAD_DIM, GQA_HEAD_DIM))
    swap[np.arange(half) + half, np.arange(half)] = 1.0
    swap[np.arange(half), np.arange(half) + half] = 1.0
    perm = np.kron(np.eye(GQA_HEADS), swap)
    place = np.zeros((GQA_WIDTH, GQA_HEADS * SLOT))
    for hd in range(GQA_HEADS):
        kv = hd // (GQA_HEADS // GQA_KV_HEADS)
        for d in range(GQA_HEAD_DIM):
            place[hd * GQA_HEAD_DIM + d, hd * SLOT + kv * GQA_HEAD_DIM + d] = 1.0
    return (jnp.asarray(bd, BF16), jnp.asarray(perm, BF16), jnp.asarray(place, BF16))


def _layer_weights(l, p):
    f = lambda name: p[name][l].astype(F32)
    lw = {}
    w_in = f('w_in')
    u, c_q, c_kv, k_pe, q_g, k_g, v_g = jnp.split(w_in, [256, 512, 640, 672, 1056, 1184], axis=-1)
    z = lambda n: jnp.zeros((D_MODEL, n), F32)
    half = MLA_ROPE // 2
    kpe_slot = jnp.concatenate([z(MLA_NOPE), k_pe, z(32)], axis=-1)
    kpe_rot = jnp.concatenate([z(MLA_NOPE), -k_pe[:, half:], k_pe[:, :half], z(32)], axis=-1)
    lw['w_in'] = jnp.concatenate([u, c_q, c_kv, kpe_slot, kpe_rot, q_g, k_g], axis=-1).astype(BF16)
    lw['w_vg_t'] = v_g.T.astype(BF16)
    lw['attn_norm'] = f('attn_norm')[None]
    lw['mla_q_norm'] = f('mla_q_norm')[None]
    lw['mla_kv_norm'] = f('mla_kv_norm')[None]
    w_uq = f('mla_w_uq').reshape(MLA_Q_LORA, MLA_HEADS, MLA_NOPE + MLA_ROPE)
    nope, pe = w_uq[..., :MLA_NOPE], w_uq[..., MLA_NOPE:]
    zq = jnp.zeros((MLA_Q_LORA, MLA_HEADS, 32), F32)
    lw['w_uq'] = jnp.concatenate([nope, pe, zq], axis=-1).reshape(MLA_Q_LORA, -1).astype(BF16)
    lw['w_uq_rot'] = jnp.concatenate([jnp.zeros_like(nope), -pe[..., half:], pe[..., :half], zq],
                                     axis=-1).reshape(MLA_Q_LORA, -1).astype(BF16)
    w_ukv = f('mla_w_ukv').reshape(MLA_KV_LORA, MLA_HEADS, MLA_NOPE + MLA_V)
    zk = jnp.zeros((MLA_KV_LORA, MLA_HEADS, SLOT - MLA_NOPE), F32)
    lw['w_ukv_k'] = jnp.concatenate([w_ukv[..., :MLA_NOPE], zk], axis=-1).reshape(MLA_KV_LORA, -1).astype(BF16)
    lw['w_ukv_vt'] = w_ukv[..., MLA_NOPE:].reshape(MLA_KV_LORA, -1).T.astype(BF16)
    lw['gqa_q_norm'] = jnp.tile(f('gqa_q_norm'), GQA_HEADS)[None]
    lw['gqa_k_norm'] = jnp.tile(f('gqa_k_norm'), GQA_KV_HEADS)[None]
    lw['bd'], lw['perm'], lw['place'] = _static_mats()
    lw['s5_toep'], lw['s5_w_s'], lw['s5_w_y'], lw['s5_coef'] = _s5_weights(
        p['s5_lam_re'][l], p['s5_lam_im'][l], p['s5_log_dt'][l], p['s5_b_re'][l], p['s5_b_im'][l],
        p['s5_c_re'][l], p['s5_c_im'][l], p['s5_d'][l])
    lw['w_glu'] = f('s5_w_glu').astype(BF16)
    lw['b_glu'] = f('s5_b_glu')[None]
    lw['out_norm_ssm'] = f('out_norm_ssm')[None]
    lw['out_norm_mla'] = f('out_norm_mla')[None]
    lw['out_norm_gqa'] = f('out_norm_gqa')[None]
    w_out = f('w_out')
    lw['w_out_s'] = w_out[:SSM_WIDTH].astype(BF16)
    lw['w_out_m'] = w_out[SSM_WIDTH:SSM_WIDTH + MLA_WIDTH].astype(BF16)
    lw['w_out_g'] = w_out[SSM_WIDTH + MLA_WIDTH:].astype(BF16)
    lw['ffn_norm'] = f('ffn_norm')[None]
    w_r = jnp.concatenate([f('router_group_w'), jnp.zeros((D_MODEL, 4), F32), f('router_expert_w'),
                           jnp.zeros((D_MODEL, LANE - 8 - N_EXPERTS), F32)], axis=-1)
    lw['w_r_hi'], lw['w_r_lo'] = _split_bf16(w_r)
    lw['r_bias'] = jnp.concatenate([f('router_group_b'), jnp.full((4,), -1e30, F32), f('router_expert_b'),
                                    jnp.zeros((LANE - 8 - N_EXPERTS,), F32)])[None]
    lw['layer'] = l
    lw['w_gate'] = p['expert_w_gate'].astype(F32)
    lw['w_up'] = p['expert_w_up'].astype(F32)
    lw['w_down'] = p['expert_w_down'].astype(F32)
    return lw


def _mixer(xf, lw, tabs, bsz, seq_len):
    u, q_m, k_m, v_m, q_g, k_g, v_g = _inproj(xf, lw, tabs, seq_len)
    y = _s5(u, lw, bsz, seq_len)
    o_m = _attention(q_m, k_m, v_m, bsz, seq_len, groups=MLA_HEADS // 2, qslots=((0,), (1,)),
                     kslots=2, kslot=(0, 1), tq=512)
    o_g = _attention(q_g, k_g, v_g, bsz, seq_len, groups=1, qslots=((0, 1, 2), (3, 4, 5)),
                     kslots=1, kslot=(0, 0), tq=256)
    return _outproj(y, o_m, o_g, xf, lw)


def _trunks(xs, layers, final_gain):
    shapes = [x.shape for x in xs]
    tabs = [_rope_tables(s[1]) for s in shapes]
    flat = [x.reshape(s[0] * s[1], D_MODEL).astype(F32) for x, s in zip(xs, shapes)]
    depth = len(layers)
    for l, lw in enumerate(layers):
        mixed = [_mixer(xf, lw, tb, s[0], s[1]) for xf, tb, s in zip(flat, tabs, shapes)]
        flat = [_moe(h, idx, wts, x_mid, lw, final_gain, final=(l == depth - 1))
                for x_mid, h, idx, wts in mixed]
    return tuple(xf.reshape(s) for xf, s in zip(flat, shapes))


def kernel(x_prompt, x_sample, attn_norm, w_in, s5_lam_re, s5_lam_im, s5_log_dt, s5_b_re, s5_b_im, s5_c_re, s5_c_im, s5_d, s5_w_glu, s5_b_glu, mla_q_norm, mla_w_uq, mla_kv_norm, mla_w_ukv, gqa_q_norm, gqa_k_norm, out_norm_ssm, out_norm_mla, out_norm_gqa, w_out, ffn_norm, router_group_w, router_group_b, router_expert_w, router_expert_b, expert_w_gate, expert_w_up, expert_w_down, final_norm):
    p = dict(attn_norm=attn_norm, w_in=w_in, s5_lam_re=s5_lam_re, s5_lam_im=s5_lam_im, s5_log_dt=s5_log_dt,
             s5_b_re=s5_b_re, s5_b_im=s5_b_im, s5_c_re=s5_c_re, s5_c_im=s5_c_im, s5_d=s5_d,
             s5_w_glu=s5_w_glu, s5_b_glu=s5_b_glu, mla_q_norm=mla_q_norm, mla_w_uq=mla_w_uq,
             mla_kv_norm=mla_kv_norm, mla_w_ukv=mla_w_ukv, gqa_q_norm=gqa_q_norm, gqa_k_norm=gqa_k_norm,
             out_norm_ssm=out_norm_ssm, out_norm_mla=out_norm_mla, out_norm_gqa=out_norm_gqa, w_out=w_out,
             ffn_norm=ffn_norm, router_group_w=router_group_w, router_group_b=router_group_b,
             router_expert_w=router_expert_w, router_expert_b=router_expert_b,
             expert_w_gate=expert_w_gate, expert_w_up=expert_w_up, expert_w_down=expert_w_down)
    depth = w_in.shape[0]
    layers = [_layer_weights(l, p) for l in range(depth)]
    final_gain = final_norm.astype(F32)[None]
    return _trunks((x_prompt, x_sample), layers, final_gain)
```

```python
import functools
import math

import jax
import jax.numpy as jnp
import numpy as np
from jax import lax
from jax.experimental import pallas as pl
from jax.experimental.pallas import tpu as pltpu

F32 = jnp.float32
BF16 = jnp.bfloat16

D_MODEL = 1024
GRID_W = 64
NORM_EPS = 1e-6
ROPE_THETA = 10000.0

SSM_WIDTH = 256
SSM_GROUP = 16
SSM_GROUPS = 16
SSM_STATE = 64

MLA_HEADS = 6
MLA_NOPE = 64
MLA_ROPE = 32
MLA_V = 64
MLA_Q_LORA = 256
MLA_KV_LORA = 128
MLA_WIDTH = MLA_HEADS * MLA_V

GQA_HEADS = 6
GQA_KV_HEADS = 2
GQA_HEAD_DIM = 64
GQA_WIDTH = GQA_HEADS * GQA_HEAD_DIM

N_EXPERT_GROUPS = 4
EXPERTS_PER_GROUP = 8
N_EXPERTS = 32
EXPERT_FF = 512

LANE = 128
SUBLANES = 8
SLOT = 128
S5_CHUNK = 64
S5_K = S5_CHUNK * SSM_GROUP
S5_SW = 512
S5_HW = 256
PACK_TOKENS = LANE // SSM_GROUP
PACK_ROWS = S5_CHUNK // PACK_TOKENS
S5_ROWS = 512
S5_CARRY_TB = 4
WIN_PAD = 1408
MOE_TILE = 512
V_ROWS = 80
ATTN_STEP_ROWS = 2048
VMEM_LIMIT = 48 * 1024 * 1024
LOG2E = math.log2(math.e)

_C_U, _C_CQ, _C_CKV, _C_KPE, _C_KPER, _C_QG, _C_KG = 0, 256, 512, 640, 768, 896, 1280


def _cparams(*sem):
    return pltpu.CompilerParams(dimension_semantics=sem, vmem_limit_bytes=VMEM_LIMIT)


def _split_bf16(x):
    hi = x.astype(BF16)
    lo = (x - hi.astype(F32)).astype(BF16)
    return hi, lo


def _dot(a, b):
    return jnp.dot(a, b, preferred_element_type=F32)


def _rms(x, gain):
    return x * lax.rsqrt(jnp.mean(x * x, axis=-1, keepdims=True) + NORM_EPS) * gain


def _pack_groups(u, u_ref, pack_ref):
    tm = u.shape[0]
    rows = tm // PACK_TOKENS
    groups_per_tile = LANE // SSM_GROUP
    pack_ref[0] = u[:, 0:LANE]
    pack_ref[1] = u[:, LANE:2 * LANE]
    lane_blk = lax.broadcasted_iota(jnp.int32, (rows, LANE), 1) // SSM_GROUP
    for half in range(SSM_WIDTH // LANE):
        rot = {}
        for tl in range(PACK_TOKENS):
            tok = pack_ref[half, pl.ds(tl, rows, stride=PACK_TOKENS), :]
            for k in range(groups_per_tile):
                rot[tl, k] = tok if k == 0 else pltpu.roll(tok, k * SSM_GROUP, 1)
        for gl in range(groups_per_tile):
            acc = rot[0, (0 - gl) % groups_per_tile]
            for tl in range(1, PACK_TOKENS):
                acc = jnp.where(lane_blk == tl, rot[tl, (tl - gl) % groups_per_tile], acc)
            u_ref[half * groups_per_tile + gl] = acc


def _unpack_groups(y_ref, pack_ref):
    groups_per_tile = LANE // SSM_GROUP
    rows = y_ref.shape[1]
    for tl in range(PACK_TOKENS):
        for half in range(SSM_WIDTH // LANE):
            cols = [y_ref[half * groups_per_tile + gl][:, tl * SSM_GROUP:(tl + 1) * SSM_GROUP]
                    for gl in range(groups_per_tile)]
            pack_ref[half, pl.ds(tl, rows, stride=PACK_TOKENS), :] = jnp.concatenate(cols, axis=-1)
    return jnp.concatenate([pack_ref[0], pack_ref[1]], axis=-1)


def _dot_nt(a, b):
    return lax.dot_general(a, b, (((1,), (1,)), ((), ())), preferred_element_type=F32)


def _store_values_t(vt, vt_ref):
    for hd in range(vt_ref.shape[0]):
        vt_ref[hd, 0:64, :] = vt[hd * 64:(hd + 1) * 64, :].astype(BF16)
        vt_ref[hd, 64:V_ROWS, :] = jnp.ones((V_ROWS - 64, vt.shape[1]), BF16)


def _inproj_kernel(x_ref, g_ref, win_ref, wvg_ref, qng_ref, wuq_ref, wuqr_ref, kvng_ref, wukk_ref, wukv_ref,
                   gqn_ref, gkn_ref, bd_ref, perm_ref, place_ref,
                   mc_ref, ms_ref, gc_ref, gs_ref,
                   u_ref, qm_ref, km_ref, vm_ref, qg_ref, kg_ref, vg_ref, pack_ref):
    h = _rms(x_ref[...], g_ref[...]).astype(BF16)
    proj = _dot(h, win_ref[...])
    _pack_groups(proj[:, _C_U:_C_U + SSM_WIDTH], u_ref, pack_ref)
    _store_values_t(_dot_nt(wvg_ref[...], h), vg_ref)

    mc = mc_ref[...]
    ms = ms_ref[...]
    cq = _rms(proj[:, _C_CQ:_C_CQ + MLA_Q_LORA], qng_ref[...]).astype(BF16)
    qa = _dot(cq, wuq_ref[...])
    qb = _dot(cq, wuqr_ref[...])
    scale = (MLA_NOPE + MLA_ROPE) ** -0.5 * LOG2E
    for hd in range(MLA_HEADS):
        sl = slice(hd * SLOT, (hd + 1) * SLOT)
        qm_ref[:, sl] = ((qa[:, sl] * mc + qb[:, sl] * ms) * scale).astype(BF16)
    ckv = _rms(proj[:, _C_CKV:_C_CKV + MLA_KV_LORA], kvng_ref[...]).astype(BF16)
    kk = _dot(ckv, wukk_ref[...])
    kpe = proj[:, _C_KPE:_C_KPE + SLOT] * mc + proj[:, _C_KPER:_C_KPER + SLOT] * ms
    for hd in range(MLA_HEADS):
        sl = slice(hd * SLOT, (hd + 1) * SLOT)
        km_ref[:, sl] = (kk[:, sl] + kpe).astype(BF16)
    _store_values_t(_dot_nt(wukv_ref[...], ckv), vm_ref)

    gc = gc_ref[...]
    gs = gs_ref[...]
    bd = bd_ref[...]
    perm = perm_ref[...]
    qg = proj[:, _C_QG:_C_QG + GQA_WIDTH]
    qn = qg * lax.rsqrt(_dot((qg * qg).astype(BF16), bd) + NORM_EPS) * gqn_ref[...]
    gc3 = jnp.concatenate([gc, gc, gc], axis=-1)
    gs3 = jnp.concatenate([gs, gs, gs], axis=-1)
    qr = (qn * gc3 + _dot(qn.astype(BF16), perm) * gs3) * (GQA_HEAD_DIM ** -0.5 * LOG2E)
    qg_ref[...] = _dot(qr.astype(BF16), place_ref[...]).astype(BF16)
    kg = proj[:, _C_KG:_C_KG + 128]
    kn = kg * lax.rsqrt(_dot((kg * kg).astype(BF16), bd[:128, :128]) + NORM_EPS) * gkn_ref[...]
    kg_ref[...] = (kn * gc + _dot(kn.astype(BF16), perm[:128, :128]) * gs).astype(BF16)


def _inproj(x, lw, tabs, seq_len):
    n_tok = x.shape[0]
    tm = min(512, seq_len)
    nt = n_tok // tm
    per_seq = seq_len // tm

    def row(i):
        return (i, 0)

    def full(i):
        return (0, 0)

    def tab(i):
        return (i % per_seq, 0)

    consts = [lw['attn_norm'], lw['w_in'], lw['w_vg_t'], lw['mla_q_norm'], lw['w_uq'], lw['w_uq_rot'],
              lw['mla_kv_norm'], lw['w_ukv_k'], lw['w_ukv_vt'], lw['gqa_q_norm'], lw['gqa_k_norm'],
              lw['bd'], lw['perm'], lw['place']]
    in_specs = [pl.BlockSpec((tm, D_MODEL), row)]
    in_specs += [pl.BlockSpec(c.shape, full) for c in consts]
    in_specs += [pl.BlockSpec((tm, LANE), tab)] * 4
    bsz = n_tok // seq_len
    u_spec = pl.BlockSpec((SSM_GROUPS, tm // PACK_TOKENS, LANE), lambda i: (0, i, 0))

    def vt_spec(n_heads):
        return pl.BlockSpec((None, n_heads, None, V_ROWS, tm), lambda i: (i // per_seq, 0, i % per_seq, 0, 0))

    def vt_shape(n_heads):
        return jax.ShapeDtypeStruct((bsz, n_heads, per_seq, V_ROWS, tm), BF16)

    def tok(w):
        return pl.BlockSpec((tm, w), row), jax.ShapeDtypeStruct((n_tok, w), BF16)

    outs = [(u_spec, jax.ShapeDtypeStruct((SSM_GROUPS, n_tok // PACK_TOKENS, LANE), F32)),
            tok(MLA_HEADS * SLOT), tok(MLA_HEADS * SLOT), (vt_spec(MLA_HEADS), vt_shape(MLA_HEADS)),
            tok(GQA_HEADS * SLOT), tok(128), (vt_spec(GQA_KV_HEADS), vt_shape(GQA_KV_HEADS))]
    return pl.pallas_call(
        _inproj_kernel,
        grid=(nt,),
        in_specs=in_specs,
        out_specs=[o[0] for o in outs],
        out_shape=[o[1] for o in outs],
        scratch_shapes=[pltpu.VMEM((SSM_WIDTH // LANE, tm, LANE), F32)],
        compiler_params=_cparams("parallel"),
        name="inproj",
    )(x, *consts, *tabs)


def _chunk_dot(u_ref, w_ref, rows):
    acc = None
    for t_hi in range(PACK_ROWS):
        lhs = u_ref[0, pl.ds(t_hi, rows, stride=PACK_ROWS), :].astype(BF16)
        part = _dot(lhs, w_ref[0, t_hi * LANE:(t_hi + 1) * LANE, :])
        acc = part if acc is None else acc + part
    return acc


def _s5_state_kernel(u_ref, w_ref, s_ref, *, n_chunks, tb):
    res = _chunk_dot(u_ref, w_ref, tb * n_chunks)
    for bb in range(tb):
        s_ref[:, bb * S5_SW:(bb + 1) * S5_SW] = res[bb * n_chunks:(bb + 1) * n_chunks]


def _s5_state(u_t, w_s, n_chunks, tb):
    n_grp = u_t.shape[0]
    rows = u_t.shape[1] // PACK_ROWS
    n_bt = rows // (tb * n_chunks)
    return pl.pallas_call(
        functools.partial(_s5_state_kernel, n_chunks=n_chunks, tb=tb),
        grid=(n_grp, n_bt),
        in_specs=[pl.BlockSpec((1, tb * n_chunks * PACK_ROWS, LANE), lambda g, r: (g, r, 0)),
                  pl.BlockSpec((1, S5_K, S5_SW), lambda g, r: (g, 0, 0))],
        out_specs=pl.BlockSpec((n_chunks, tb * S5_SW), lambda g, r: (0, g * n_bt + r)),
        out_shape=jax.ShapeDtypeStruct((n_chunks, n_grp * n_bt * tb * S5_SW), F32),
        compiler_params=_cparams("parallel", "parallel"),
        name="s5_state",
    )(u_t, w_s)


def _s5_carry_kernel(s_ref, c_ref, h_ref, *, n_chunks, tb):
    a1f, a2f, a3f = c_ref[0, 0:1, 0:LANE], c_ref[0, 1:2, 0:LANE], c_ref[0, 2:3, 0:LANE]
    a1b, a2b, a3b = c_ref[0, 0:1, LANE:], c_ref[0, 1:2, LANE:], c_ref[0, 2:3, LANE:]
    zero = jnp.zeros((1, LANE), F32)
    th = SUBLANES if n_chunks % SUBLANES == 0 else n_chunks
    n_tiles = n_chunks // th
    row = lax.broadcasted_iota(jnp.int32, (th, LANE), 0)

    def body(it, carry):
        base_f = pl.multiple_of(it * th, th)
        base_b = pl.multiple_of((n_tiles - 1 - it) * th, th)
        new = []
        for bb in range(tb):
            hf, hfs, hb, hbs = carry[4 * bb:4 * bb + 4]
            s0 = bb * S5_SW
            h0 = bb * S5_HW
            sf = s_ref[pl.ds(base_f, th), s0:s0 + LANE]
            sfs = s_ref[pl.ds(base_f, th), s0 + LANE:s0 + 2 * LANE]
            sb = s_ref[pl.ds(base_b, th), s0 + 2 * LANE:s0 + 3 * LANE]
            sbs = s_ref[pl.ds(base_b, th), s0 + 3 * LANE:s0 + 4 * LANE]
            out_f = jnp.zeros((th, LANE), F32)
            out_b = jnp.zeros((th, LANE), F32)
            for r in range(th):
                rb = th - 1 - r
                out_f = jnp.where(row == r, hf, out_f)
                out_b = jnp.where(row == rb, hb, out_b)
                hf, hfs = (a1f * hf + a2f * hfs + sf[r:r + 1], a1f * hfs + a3f * hf + sfs[r:r + 1])
                hb, hbs = (a1b * hb + a2b * hbs + sb[rb:rb + 1], a1b * hbs + a3b * hb + sbs[rb:rb + 1])
            h_ref[pl.ds(base_f, th), h0:h0 + LANE] = out_f
            h_ref[pl.ds(base_b, th), h0 + LANE:h0 + 2 * LANE] = out_b
            new += [hf, hfs, hb, hbs]
        return tuple(new)

    lax.fori_loop(0, n_tiles, body, (zero,) * (4 * tb))


def _s5_carry(s, coef, n_chunks, tb):
    n_grp = coef.shape[0]
    n_bt = s.shape[1] // (n_grp * tb * S5_SW)
    return pl.pallas_call(
        functools.partial(_s5_carry_kernel, n_chunks=n_chunks, tb=tb),
        grid=(n_grp, n_bt),
        in_specs=[pl.BlockSpec((n_chunks, tb * S5_SW), lambda g, r: (0, g * n_bt + r)),
                  pl.BlockSpec((1, 3, S5_HW), lambda g, r: (g, 0, 0))],
        out_specs=pl.BlockSpec((n_chunks, tb * S5_HW), lambda g, r: (0, g * n_bt + r)),
        out_shape=jax.ShapeDtypeStruct((n_chunks, n_grp * n_bt * tb * S5_HW), F32),
        compiler_params=_cparams("parallel", "parallel"),
        name="s5_carry",
    )(s, coef)


def _gelu_tanh(y):
    return 0.5 * y * (1.0 + jnp.tanh(math.sqrt(2.0 / math.pi) * (y + 0.044715 * (y * y * y))))


def _s5_out_kernel(u_ref, h_ref, toep_ref, wy_ref, y_ref, *, n_chunks, tb):
    rows = tb * n_chunks
    h = jnp.concatenate([h_ref[:, bb * S5_HW:(bb + 1) * S5_HW] for bb in range(tb)], axis=0)
    y = _gelu_tanh(_chunk_dot(u_ref, toep_ref, rows) + _dot(h.astype(BF16), wy_ref[0]))
    for t_hi in range(PACK_ROWS):
        y_ref[0, pl.ds(t_hi, rows, stride=PACK_ROWS), :] = y[:, t_hi * LANE:(t_hi + 1) * LANE]


def _s5_out(u_t, h_in, toep, w_y, n_chunks, tb):
    n_grp = u_t.shape[0]
    rows = u_t.shape[1] // PACK_ROWS
    n_bt = rows // (tb * n_chunks)
    blk = (1, tb * n_chunks * PACK_ROWS, LANE)
    return pl.pallas_call(
        functools.partial(_s5_out_kernel, n_chunks=n_chunks, tb=tb),
        grid=(n_grp, n_bt),
        in_specs=[pl.BlockSpec(blk, lambda g, r: (g, r, 0)),
                  pl.BlockSpec((n_chunks, tb * S5_HW), lambda g, r: (0, g * n_bt + r)),
                  pl.BlockSpec((1, S5_K, S5_K), lambda g, r: (g, 0, 0)),
                  pl.BlockSpec((1, S5_HW, S5_K), lambda g, r: (g, 0, 0))],
        out_specs=pl.BlockSpec(blk, lambda g, r: (g, r, 0)),
        out_shape=jax.ShapeDtypeStruct(u_t.shape, F32),
        compiler_params=_cparams("parallel", "parallel"),
        name="s5_out",
    )(u_t, h_in, toep, w_y)


def _s5(u, lw, bsz, seq_len):
    n_chunks = seq_len // S5_CHUNK
    tb = max(1, min(bsz, S5_ROWS // n_chunks))
    assert bsz % tb == 0
    s = _s5_state(u, lw['s5_w_s'], n_chunks, tb)
    h_in = _s5_carry(s, lw['s5_coef'], n_chunks, math.gcd(tb, S5_CARRY_TB))
    return _s5_out(u, h_in, lw['s5_toep'], lw['s5_w_y'], n_chunks, tb)


def _attn_kernel(q_ref, k_ref, vt_ref, o_ref, m_ref, a_ref, acc_ref, s_ref, *, qslots, kslot, tq, tk, n_kv):
    n_virt = len(qslots)
    n_qb = q_ref.shape[0] // tq
    m_ref[...] = jnp.full(m_ref.shape, -jnp.inf, F32)
    acc_ref[...] = jnp.zeros(acc_ref.shape, F32)

    def scores(j, qi, ki, first):
        q0 = pl.multiple_of(qi * tq, tq)
        q = jnp.concatenate([q_ref[pl.ds(q0, tq), s * SLOT:(s + 1) * SLOT] for s in qslots[j]], axis=0)
        start = pl.multiple_of(ki * tk, tk)
        kb = k_ref[pl.ds(start, tk), kslot[j] * SLOT:(kslot[j] + 1) * SLOT]
        st = lax.dot_general(kb, q, (((1,), (1,)), ((), ())), preferred_element_type=F32)
        s_ref[j] = st
        m_old = jnp.full(m_ref.shape[1:], -jnp.inf, F32) if first else m_ref[j]
        m_new = jnp.maximum(m_old, jnp.max(st, axis=0, keepdims=True))
        m_ref[j] = m_new
        a_ref[j] = jnp.exp2(m_old - m_new)

    def values(j, ki):
        pt = jnp.exp2(s_ref[j] - m_ref[j]).astype(BF16)
        acc_ref[j] = a_ref[j] * acc_ref[j] + _dot(vt_ref[j, ki], pt)

    def finish(qi):
        pieces = []
        for j in range(n_virt):
            acc = acc_ref[j]
            o = acc[0:64] / acc[64:65]
            for s in range(len(qslots[j])):
                pieces.append(o[:, s * tq:(s + 1) * tq])
        q0 = pl.multiple_of(qi * tq, tq)
        for a in range(len(pieces) // 2):
            pair = jnp.concatenate([pieces[2 * a], pieces[2 * a + 1]], axis=0)
            o_ref[pl.ds(q0, tq), a * SLOT:(a + 1) * SLOT] = pair.T

    scores(0, 0, 0, False)

    if n_qb == 1:
        def body(ki, carry):
            scores(1, 0, ki, False)
            values(0, ki)
            scores(0, 0, ki + 1, False)
            values(1, ki)
            return carry

        lax.fori_loop(0, n_kv - 1, body, 0, unroll=True if n_kv <= 5 else 4)
        scores(1, 0, n_kv - 1, False)
        values(0, n_kv - 1)
        values(1, n_kv - 1)
        finish(0)
    else:
        def qbody(qi, carry):
            for ki in range(n_kv):
                scores(1, qi, ki, ki == 0)
                values(0, ki)
                if ki + 1 < n_kv:
                    scores(0, qi, ki + 1, False)
                else:
                    scores(0, jnp.minimum(qi + 1, n_qb - 1), 0, True)
                values(1, ki)
            finish(qi)
            return carry

        lax.fori_loop(0, n_qb, qbody, 0, unroll=2 if n_qb % 2 == 0 else 1)


def _attention(q, k, vt, bsz, seq_len, *, groups, qslots, kslots, kslot, tq):
    tq = min(tq, seq_len)
    n_kv, tk = vt.shape[2], vt.shape[4]
    rows = seq_len if seq_len <= ATTN_STEP_ROWS else tq
    nq = seq_len // rows
    heads = sum(len(s) for s in qslots)
    n_stack = len(qslots[0])
    q3 = q.reshape(bsz, seq_len, -1)
    k3 = k.reshape(bsz, seq_len, -1)
    ow = heads * 64
    out = pl.pallas_call(
        functools.partial(_attn_kernel, qslots=qslots, kslot=kslot, tq=tq, tk=tk, n_kv=n_kv),
        grid=(bsz, groups, nq),
        in_specs=[pl.BlockSpec((None, rows, heads * SLOT), lambda b, g, i: (b, i, g)),
                  pl.BlockSpec((None, seq_len, kslots * SLOT), lambda b, g, i: (b, 0, g)),
                  pl.BlockSpec((None, 2, n_kv, V_ROWS, tk), lambda b, g, i: (b, g, 0, 0, 0))],
        out_specs=pl.BlockSpec((None, rows, ow), lambda b, g, i: (b, i, g)),
        out_shape=jax.ShapeDtypeStruct((bsz, seq_len, groups * ow), F32),
        scratch_shapes=[pltpu.VMEM((2, 1, n_stack * tq), F32), pltpu.VMEM((2, 1, n_stack * tq), F32),
                        pltpu.VMEM((2, V_ROWS, n_stack * tq), F32), pltpu.VMEM((2, tk, n_stack * tq), F32)],
        compiler_params=_cparams("parallel", "parallel", "arbitrary"),
        name="attention",
    )(q3, k3, vt)
    return out.reshape(bsz * seq_len, groups * ow)


def _outproj_kernel(y_ref, om_ref, og_ref, x_ref, wglu_ref, bglu_ref, ns_ref, nm_ref, ng_ref,
                    wos_ref, wom_ref, wog_ref, fn_ref, wrh_ref, wrl_ref, rb_ref,
                    xo_ref, h_ref, idx_ref, wt_ref, pack_ref):
    y = _unpack_groups(y_ref, pack_ref)
    gate = jax.nn.sigmoid(_dot(y.astype(BF16), wglu_ref[...]) + bglu_ref[...])
    o_s = _rms(y * gate, ns_ref[...]).astype(BF16)
    o_m = _rms(om_ref[...], nm_ref[...]).astype(BF16)
    o_g = _rms(og_ref[...], ng_ref[...]).astype(BF16)
    x_new = x_ref[...] + _dot(o_s, wos_ref[...]) + _dot(o_m, wom_ref[...]) + _dot(o_g, wog_ref[...])
    xo_ref[...] = x_new
    h = _rms(x_new, fn_ref[...])
    h_hi, h_lo = _split_bf16(h)
    h_ref[...] = h_hi
    logits = _dot(h_hi, wrh_ref[...]) + _dot(h_lo, wrh_ref[...]) + _dot(h_hi, wrl_ref[...]) + rb_ref[...]
    lt = logits.T
    row = lax.broadcasted_iota(jnp.int32, (8, lt.shape[1]), 0)

    def first_argmax(v, vmax):
        return jnp.min(jnp.where(v == vmax, row, 8), axis=0, keepdims=True)

    gl = lt[0:8]
    gmax = jnp.max(gl, axis=0, keepdims=True)
    g_w = 1.0 / jnp.sum(jnp.exp(gl - gmax), axis=0, keepdims=True)
    g_idx = first_argmax(gl, gmax)
    el = lt[8:16]
    for g in range(1, N_EXPERT_GROUPS):
        el = jnp.where(g_idx == g, lt[8 + 8 * g:16 + 8 * g], el)
    ee = jnp.exp(el - jnp.max(el, axis=0, keepdims=True))
    ep = ee / jnp.sum(ee, axis=0, keepdims=True)
    p1 = jnp.max(ep, axis=0, keepdims=True)
    i1 = first_argmax(ep, p1)
    ep2 = jnp.where(row == i1, -1.0, ep)
    p2 = jnp.max(ep2, axis=0, keepdims=True)
    i2 = first_argmax(ep2, p2)
    denom = p1 + p2
    idx_ref[0:1, :] = g_idx * EXPERTS_PER_GROUP + i1
    idx_ref[1:2, :] = g_idx * EXPERTS_PER_GROUP + i2
    wt_ref[0:1, :] = g_w * p1 / denom
    wt_ref[1:2, :] = g_w * p2 / denom


def _outproj(y, o_m, o_g, x, lw):
    n_tok = x.shape[0]
    tm = min(512, n_tok)

    def row(i):
        return (i, 0)

    def full(i):
        return (0, 0)

    consts = [lw['w_glu'], lw['b_glu'], lw['out_norm_ssm'], lw['out_norm_mla'], lw['out_norm_gqa'],
              lw['w_out_s'], lw['w_out_m'], lw['w_out_g'], lw['ffn_norm'],
              lw['w_r_hi'], lw['w_r_lo'], lw['r_bias']]
    in_specs = [pl.BlockSpec((SSM_GROUPS, tm // PACK_TOKENS, LANE), lambda i: (0, i, 0)),
                pl.BlockSpec((tm, MLA_WIDTH), row),
                pl.BlockSpec((tm, GQA_WIDTH), row), pl.BlockSpec((tm, D_MODEL), row)]
    in_specs += [pl.BlockSpec(c.shape, full) for c in consts]
    return pl.pallas_call(
        _outproj_kernel,
        grid=(n_tok // tm,),
        in_specs=in_specs,
        out_specs=[pl.BlockSpec((tm, D_MODEL), row), pl.BlockSpec((tm, D_MODEL), row),
                   pl.BlockSpec((2, tm), lambda i: (0, i)), pl.BlockSpec((2, tm), lambda i: (0, i))],
        out_shape=[jax.ShapeDtypeStruct((n_tok, D_MODEL), F32), jax.ShapeDtypeStruct((n_tok, D_MODEL), BF16),
                   jax.ShapeDtypeStruct((2, n_tok), jnp.int32), jax.ShapeDtypeStruct((2, n_tok), F32)],
        scratch_shapes=[pltpu.VMEM((SSM_WIDTH // LANE, tm, LANE), F32)],
        compiler_params=_cparams("parallel"),
        name="outproj_router",
    )(y, o_m, o_g, x, *consts)


def _expert_kernel(be_ref, nu_ref, x_ref, wg_ref, wu_ref, wd_ref, o_ref, wg_s, wu_s, wd_s):
    i = pl.program_id(0)

    @pl.when(jnp.logical_or(i == 0, be_ref[i] != be_ref[jnp.maximum(i - 1, 0)]))
    def _():
        wg_s[...] = wg_ref[0].astype(BF16)
        wu_s[...] = wu_ref[0].astype(BF16)
        wd_s[...] = wd_ref[0].astype(BF16)

    @pl.when(i < nu_ref[0])
    def _():
        xb = x_ref[...]
        hg = _dot(xb, wg_s[...])
        hu = _dot(xb, wu_s[...])
        act = (hg * jax.nn.sigmoid(hg) * hu).astype(BF16)
        o_ref[...] = _dot(act, wd_s[...]).astype(BF16)

    @pl.when(i >= nu_ref[0])
    def _():
        o_ref[...] = jnp.zeros_like(o_ref)


def _experts(buf, block_expert, n_used, layer, w_gate, w_up, w_down):
    rows = buf.shape[0]
    n_blocks = rows // MOE_TILE
    grid_spec = pltpu.PrefetchScalarGridSpec(
        num_scalar_prefetch=2,
        grid=(n_blocks,),
        in_specs=[pl.BlockSpec((MOE_TILE, D_MODEL), lambda i, be, nu: (i, 0)),
                  pl.BlockSpec((None, 1, D_MODEL, EXPERT_FF), lambda i, be, nu: (layer, be[i], 0, 0)),
                  pl.BlockSpec((None, 1, D_MODEL, EXPERT_FF), lambda i, be, nu: (layer, be[i], 0, 0)),
                  pl.BlockSpec((None, 1, EXPERT_FF, D_MODEL), lambda i, be, nu: (layer, be[i], 0, 0))],
        out_specs=pl.BlockSpec((MOE_TILE, D_MODEL), lambda i, be, nu: (i, 0)),
        scratch_shapes=[pltpu.VMEM((D_MODEL, EXPERT_FF), BF16), pltpu.VMEM((D_MODEL, EXPERT_FF), BF16),
                        pltpu.VMEM((EXPERT_FF, D_MODEL), BF16)],
    )
    return pl.pallas_call(
        _expert_kernel,
        grid_spec=grid_spec,
        out_shape=jax.ShapeDtypeStruct((rows, D_MODEL), BF16),
        compiler_params=_cparams("arbitrary"),
        name="experts",
    )(block_expert, n_used, buf, w_gate, w_up, w_down)


def _combine_kernel(x_ref, g0_ref, g1_ref, w_ref, gain_ref, o_ref, *, final):
    w = w_ref[...]
    y = x_ref[...] + w[:, 0:1] * g0_ref[...].astype(F32) + w[:, 1:2] * g1_ref[...].astype(F32)
    if final:
        y = _rms(y, gain_ref[...])
    o_ref[...] = y


def _combine(x, g0, g1, w_col, gain, final):
    n_tok = x.shape[0]
    tm = min(512, n_tok)

    def row(i):
        return (i, 0)

    return pl.pallas_call(
        functools.partial(_combine_kernel, final=final),
        grid=(n_tok // tm,),
        in_specs=[pl.BlockSpec((tm, D_MODEL), row), pl.BlockSpec((tm, D_MODEL), row),
                  pl.BlockSpec((tm, D_MODEL), row), pl.BlockSpec((tm, 2), row),
                  pl.BlockSpec((1, D_MODEL), lambda i: (0, 0))],
        out_specs=pl.BlockSpec((tm, D_MODEL), row),
        out_shape=jax.ShapeDtypeStruct((n_tok, D_MODEL), F32),
        compiler_params=_cparams("parallel"),
        name="moe_combine",
    )(x, g0, g1, w_col, gain)


def _moe(h, idx, wts, x, lw, gain, final):
    n_tok = h.shape[0]
    n_assign = 2 * n_tok
    i32 = jnp.int32
    flat_e = idx.reshape(-1)
    experts = jnp.arange(N_EXPERTS, dtype=i32)
    order = jnp.argsort(flat_e).astype(i32)
    counts = jnp.sum((flat_e[:, None] == experts[None, :]).astype(i32), axis=0)
    padded = (counts + MOE_TILE - 1) // MOE_TILE * MOE_TILE
    end = jnp.cumsum(counts)
    start = end - counts
    pad_end = jnp.cumsum(padded)
    pad_start = pad_end - padded
    n_blocks = -(-n_assign // MOE_TILE) + N_EXPERTS
    rows = n_blocks * MOE_TILE
    blk0 = jnp.arange(n_blocks, dtype=i32) * MOE_TILE
    block_expert = jnp.minimum(jnp.sum((pad_end[None, :] <= blk0[:, None]).astype(i32), axis=1), N_EXPERTS - 1)
    n_used = (pad_end[-1] // MOE_TILE).astype(i32).reshape(1)
    row_e = jnp.repeat(block_expert, MOE_TILE)
    local = jnp.arange(rows, dtype=i32) - pad_start[row_e]
    sorted_idx = jnp.clip(start[row_e] + local, 0, n_assign - 1)
    row_id = jnp.arange(rows, dtype=i32)
    src_tok = jnp.where(local < counts[row_e], order[sorted_idx] % n_tok, row_id % n_tok)
    sorted_pos = jnp.arange(n_assign, dtype=i32)
    sorted_e = jnp.sum((end[None, :] <= sorted_pos[:, None]).astype(i32), axis=1)
    dest = pad_start[sorted_e] + sorted_pos - start[sorted_e]
    _, pos = lax.sort_key_val(order, dest)
    buf = h.at[src_tok].get(mode='promise_in_bounds')
    out = _experts(buf, block_expert, n_used, lw['layer'], lw['w_gate'], lw['w_up'], lw['w_down'])
    g0 = out.at[pos[:n_tok]].get(mode='promise_in_bounds')
    g1 = out.at[pos[n_tok:]].get(mode='promise_in_bounds')
    return _combine(x, g0, g1, wts.T, gain, final)


def _rope_tables(seq_len):
    t = jnp.arange(seq_len)
    row_idx = (t // GRID_W).astype(F32)
    col_idx = (t % GRID_W).astype(F32)

    def cos_sin(rot_dim):
        n_freq = rot_dim // 4
        inv_freq = ROPE_THETA ** (-jnp.arange(n_freq, dtype=F32) / n_freq)
        ang = jnp.concatenate([row_idx[:, None] * inv_freq, col_idx[:, None] * inv_freq], axis=-1)
        return jnp.cos(ang), jnp.sin(ang)

    cm, sm = cos_sin(MLA_ROPE)
    cg, sg = cos_sin(GQA_HEAD_DIM)
    one = jnp.ones((seq_len, MLA_NOPE), F32)
    zero = jnp.zeros((seq_len, MLA_NOPE), F32)
    pad = jnp.zeros((seq_len, SLOT - MLA_NOPE - MLA_ROPE), F32)
    mla_c = jnp.concatenate([one, cm, cm, pad], axis=-1)
    mla_s = jnp.concatenate([zero, sm, sm, pad], axis=-1)
    gqa_c = jnp.concatenate([cg, cg, cg, cg], axis=-1)
    gqa_s = jnp.concatenate([-sg, sg, -sg, sg], axis=-1)
    return mla_c, mla_s, gqa_c, gqa_s


def _s5_weights(lam_re, lam_im, log_dt, b_re, b_im, c_re, c_im, d_skip):
    t_len = S5_CHUNK
    lam = lax.complex(lam_re.astype(F32), lam_im.astype(F32))
    dt = jnp.exp(log_dt.astype(F32))[..., None]
    lam_dt = lam * dt
    lam_bar = jnp.exp(lam_dt)
    b_bar = ((lam_bar - 1.0) / lam)[..., None] * lax.complex(b_re.astype(F32), b_im.astype(F32))
    c_mat = lax.complex(c_re.astype(F32), c_im.astype(F32))
    k_idx = jnp.arange(t_len + 1, dtype=F32)
    pw = jnp.exp(lam_dt[:, None] * k_idx[None, :, None, None])
    kern = jnp.einsum('dgcp,dkgp,dgpe->dkgce', c_mat, pw[:, :t_len], b_bar).real
    d_eye = jnp.eye(SSM_GROUP, dtype=F32) * d_skip.astype(F32).reshape(SSM_GROUPS, SSM_GROUP, 1)
    t_i = jnp.arange(t_len)
    table = jnp.concatenate([kern[1][:0:-1], (kern[0][0] + kern[1][0] + d_eye)[None], kern[0][1:]], axis=0)
    table = table.transpose(1, 3, 0, 2).reshape(SSM_GROUPS, SSM_GROUP, (2 * t_len - 1) * SSM_GROUP).astype(BF16)
    period = 2 * t_len * SSM_GROUP
    table = jnp.pad(table, ((0, 0), (0, 0), (0, SSM_GROUP)))
    table = jnp.roll(table, -(t_len - 1) * SSM_GROUP, axis=-1)
    flat = jnp.tile(table, (1, 1, t_len))[:, :, :t_len * (period - SSM_GROUP)]
    toep = flat.reshape(SSM_GROUPS, SSM_GROUP, t_len, period - SSM_GROUP)[..., :S5_K]
    toep = toep.transpose(0, 2, 1, 3).reshape(SSM_GROUPS, S5_K, S5_K)
    wf = pw[0][t_len - 1 - t_i][:, :, :, None] * b_bar[0][None]
    wb = pw[1][t_i][:, :, :, None] * b_bar[1][None]

    def state_cols(w):
        w = w.transpose(1, 0, 3, 2).reshape(SSM_GROUPS, S5_K, SSM_STATE)
        return jnp.concatenate([w.real, w.imag, w.imag, w.real], axis=-1)

    w_s = jnp.concatenate([state_cols(wf), state_cols(wb)], axis=-1)
    yf = c_mat[0][None] * pw[0][t_i + 1][:, :, None, :]
    yb = c_mat[1][None] * pw[1][t_len - t_i][:, :, None, :]

    def out_rows(w):
        w = w.transpose(1, 3, 0, 2).reshape(SSM_GROUPS, SSM_STATE, S5_K)
        return jnp.concatenate([w.real, -w.imag], axis=1)

    w_y = jnp.concatenate([out_rows(yf), out_rows(yb)], axis=1)
    a = pw[:, t_len]
    ar, ai = a.real, a.imag

    def coef(d):
        return jnp.stack([jnp.concatenate([ar[d], ar[d]], -1), jnp.concatenate([-ai[d], ai[d]], -1),
                          jnp.concatenate([ai[d], -ai[d]], -1)], axis=1)

    s5_coef = jnp.concatenate([coef(0), coef(1)], axis=-1)
    return toep.astype(BF16), w_s.astype(BF16), w_y.astype(BF16), s5_coef


def _static_mats():
    bd = np.kron(np.eye(GQA_HEADS), np.full((GQA_HEAD_DIM, GQA_HEAD_DIM), 1.0 / GQA_HEAD_DIM))
    half = GQA_HEAD_DIM // 2
    swap = np.zeros((GQA_HEAD_DIM, GQA_HEAD_DIM))
    swap[np.arange(half) + half, np.arange(half)] = 1.0
    swap[np.arange(half), np.arange(half) + half] = 1.0
    perm = np.kron(np.eye(GQA_HEADS), swap)
    place = np.zeros((GQA_WIDTH, GQA_HEADS * SLOT))
    for hd in range(GQA_HEADS):
        kv = hd // (GQA_HEADS // GQA_KV_HEADS)
        for d in range(GQA_HEAD_DIM):
            place[hd * GQA_HEAD_DIM + d, hd * SLOT + kv * GQA_HEAD_DIM + d] = 1.0
    return (jnp.asarray(bd, BF16), jnp.asarray(perm, BF16), jnp.asarray(place, BF16))


def _layer_weights(l, p):
    f = lambda name: p[name][l].astype(F32)
    lw = {}
    w_in = f('w_in')
    u, c_q, c_kv, k_pe, q_g, k_g, v_g = jnp.split(w_in, [256, 512, 640, 672, 1056, 1184], axis=-1)
    z = lambda n: jnp.zeros((D_MODEL, n), F32)
    half = MLA_ROPE // 2
    kpe_slot = jnp.concatenate([z(MLA_NOPE), k_pe, z(32)], axis=-1)
    kpe_rot = jnp.concatenate([z(MLA_NOPE), -k_pe[:, half:], k_pe[:, :half], z(32)], axis=-1)
    lw['w_in'] = jnp.concatenate([u, c_q, c_kv, kpe_slot, kpe_rot, q_g, k_g], axis=-1).astype(BF16)
    lw['w_vg_t'] = v_g.T.astype(BF16)
    lw['attn_norm'] = f('attn_norm')[None]
    lw['mla_q_norm'] = f('mla_q_norm')[None]
    lw['mla_kv_norm'] = f('mla_kv_norm')[None]
    w_uq = f('mla_w_uq').reshape(MLA_Q_LORA, MLA_HEADS, MLA_NOPE + MLA_ROPE)
    nope, pe = w_uq[..., :MLA_NOPE], w_uq[..., MLA_NOPE:]
    zq = jnp.zeros((MLA_Q_LORA, MLA_HEADS, 32), F32)
    lw['w_uq'] = jnp.concatenate([nope, pe, zq], axis=-1).reshape(MLA_Q_LORA, -1).astype(BF16)
    lw['w_uq_rot'] = jnp.concatenate([jnp.zeros_like(nope), -pe[..., half:], pe[..., :half], zq],
                                     axis=-1).reshape(MLA_Q_LORA, -1).astype(BF16)
    w_ukv = f('mla_w_ukv').reshape(MLA_KV_LORA, MLA_HEADS, MLA_NOPE + MLA_V)
    zk = jnp.zeros((MLA_KV_LORA, MLA_HEADS, SLOT - MLA_NOPE), F32)
    lw['w_ukv_k'] = jnp.concatenate([w_ukv[..., :MLA_NOPE], zk], axis=-1).reshape(MLA_KV_LORA, -1).astype(BF16)
    lw['w_ukv_vt'] = w_ukv[..., MLA_NOPE:].reshape(MLA_KV_LORA, -1).T.astype(BF16)
    lw['gqa_q_norm'] = jnp.tile(f('gqa_q_norm'), GQA_HEADS)[None]
    lw['gqa_k_norm'] = jnp.tile(f('gqa_k_norm'), GQA_KV_HEADS)[None]
    lw['bd'], lw['perm'], lw['place'] = _static_mats()
    lw['s5_toep'], lw['s5_w_s'], lw['s5_w_y'], lw['s5_coef'] = _s5_weights(
        p['s5_lam_re'][l], p['s5_lam_im'][l], p['s5_log_dt'][l], p['s5_b_re'][l], p['s5_b_im'][l],
        p['s5_c_re'][l], p['s5_c_im'][l], p['s5_d'][l])
    lw['w_glu'] = f('s5_w_glu').astype(BF16)
    lw['b_glu'] = f('s5_b_glu')[None]
    lw['out_norm_ssm'] = f('out_norm_ssm')[None]
    lw['out_norm_mla'] = f('out_norm_mla')[None]
    lw['out_norm_gqa'] = f('out_norm_gqa')[None]
    w_out = f('w_out')
    lw['w_out_s'] = w_out[:SSM_WIDTH].astype(BF16)
    lw['w_out_m'] = w_out[SSM_WIDTH:SSM_WIDTH + MLA_WIDTH].astype(BF16)
    lw['w_out_g'] = w_out[SSM_WIDTH + MLA_WIDTH:].astype(BF16)
    lw['ffn_norm'] = f('ffn_norm')[None]
    w_r = jnp.concatenate([f('router_group_w'), jnp.zeros((D_MODEL, 4), F32), f('router_expert_w'),
                           jnp.zeros((D_MODEL, LANE - 8 - N_EXPERTS), F32)], axis=-1)
    lw['w_r_hi'], lw['w_r_lo'] = _split_bf16(w_r)
    lw['r_bias'] = jnp.concatenate([f('router_group_b'), jnp.full((4,), -1e30, F32), f('router_expert_b'),
                                    jnp.zeros((LANE - 8 - N_EXPERTS,), F32)])[None]
    lw['layer'] = l
    lw['w_gate'] = p['expert_w_gate'].astype(F32)
    lw['w_up'] = p['expert_w_up'].astype(F32)
    lw['w_down'] = p['expert_w_down'].astype(F32)
    return lw


def _mixer(xf, lw, tabs, bsz, seq_len):
    u, q_m, k_m, v_m, q_g, k_g, v_g = _inproj(xf, lw, tabs, seq_len)
    y = _s5(u, lw, bsz, seq_len)
    o_m = _attention(q_m, k_m, v_m, bsz, seq_len, groups=MLA_HEADS // 2, qslots=((0,), (1,)),
                     kslots=2, kslot=(0, 1), tq=512)
    o_g = _attention(q_g, k_g, v_g, bsz, seq_len, groups=1, qslots=((0, 1, 2), (3, 4, 5)),
                     kslots=1, kslot=(0, 0), tq=256)
    return _outproj(y, o_m, o_g, xf, lw)


def _trunks(xs, layers, final_gain):
    shapes = [x.shape for x in xs]
    tabs = [_rope_tables(s[1]) for s in shapes]
    flat = [x.reshape(s[0] * s[1], D_MODEL).astype(F32) for x, s in zip(xs, shapes)]
    depth = len(layers)
    for l, lw in enumerate(layers):
        mixed = [_mixer(xf, lw, tb, s[0], s[1]) for xf, tb, s in zip(flat, tabs, shapes)]
        flat = [_moe(h, idx, wts, x_mid, lw, final_gain, final=(l == depth - 1))
                for x_mid, h, idx, wts in mixed]
    return tuple(xf.reshape(s) for xf, s in zip(flat, shapes))


def kernel(x_prompt, x_sample, attn_norm, w_in, s5_lam_re, s5_lam_im, s5_log_dt, s5_b_re, s5_b_im, s5_c_re, s5_c_im, s5_d, s5_w_glu, s5_b_glu, mla_q_norm, mla_w_uq, mla_kv_norm, mla_w_ukv, gqa_q_norm, gqa_k_norm, out_norm_ssm, out_norm_mla, out_norm_gqa, w_out, ffn_norm, router_group_w, router_group_b, router_expert_w, router_expert_b, expert_w_gate, expert_w_up, expert_w_down, final_norm):
    p = dict(attn_norm=attn_norm, w_in=w_in, s5_lam_re=s5_lam_re, s5_lam_im=s5_lam_im, s5_log_dt=s5_log_dt,
             s5_b_re=s5_b_re, s5_b_im=s5_b_im, s5_c_re=s5_c_re, s5_c_im=s5_c_im, s5_d=s5_d,
             s5_w_glu=s5_w_glu, s5_b_glu=s5_b_glu, mla_q_norm=mla_q_norm, mla_w_uq=mla_w_uq,
             mla_kv_norm=mla_kv_norm, mla_w_ukv=mla_w_ukv, gqa_q_norm=gqa_q_norm, gqa_k_norm=gqa_k_norm,
             out_norm_ssm=out_norm_ssm, out_norm_mla=out_norm_mla, out_norm_gqa=out_norm_gqa, w_out=w_out,
             ffn_norm=ffn_norm, router_group_w=router_group_w, router_group_b=router_group_b,
             router_expert_w=router_expert_w, router_expert_b=router_expert_b,
             expert_w_gate=expert_w_gate, expert_w_up=expert_w_up, expert_w_down=expert_w_down)
    depth = w_in.shape[0]
    layers = [_layer_weights(l, p) for l in range(depth)]
    final_gain = final_norm.astype(F32)[None]
    return _trunks((x_prompt, x_sample), layers, final_gain)
```
